```python
import math
import jax
import jax.numpy as jnp
from jax import lax
import numpy as np

D_MODEL = 1024
BATCH = 2
SEQ = 8192
DEPTH = 2

N_MIXERS = 2
Q_BLOCK = 128
NORM_EPS = 1e-6
FOX_HEADS = 16
FOX_HEAD_DIM = D_MODEL // FOX_HEADS
FOX_WIDTH = FOX_HEADS * FOX_HEAD_DIM
MLA_HEADS = 16
MLA_NOPE_DIM = 64
MLA_ROPE_DIM = 32
MLA_V_DIM = 64
MLA_Q_RANK = 384
MLA_KV_RANK = 256
ROPE_BASE = 10000.0
D_FF = 2816
CONV_WIDTH = 3
N_FOX_LAYERS = (DEPTH + 1) // 2
N_MLA_LAYERS = DEPTH // 2

kernel_name = 'hybrid_fox_mla_convffn_adaln'


def rms_norm(x):
    xf = x.astype(jnp.float32)
    y = xf * lax.rsqrt(jnp.mean(xf * xf, axis=-1, keepdims=True) + NORM_EPS)
    return y.astype(x.dtype)


def ada_modulation(c, w, b):
    mod = jax.nn.silu(c) @ w + b
    shift, scale, gate = jnp.split(mod[:, None, :], 3, axis=-1)
    return shift, scale, gate


def causal_mask(start, seq):
    q_pos = start + jnp.arange(Q_BLOCK)
    return q_pos[:, None] >= jnp.arange(seq)[None, :]


def masked_softmax(s, mask):
    s = jnp.where(mask, s.astype(jnp.float32), -jnp.inf)
    return jax.nn.softmax(s, axis=-1)


def causal_block_sweep(block_fn, batch, seq):
    starts = jnp.arange(seq // Q_BLOCK, dtype=jnp.int32) * Q_BLOCK
    out = lax.map(block_fn, starts)
    return jnp.moveaxis(out, 0, 1).reshape(batch, seq, -1)


def forgetting_attention(h, w_in, b_f, w_o):
    B, S, _ = h.shape
    proj = h @ w_in
    q, k, v, f_logit = jnp.split(proj, [FOX_WIDTH, 2 * FOX_WIDTH, 3 * FOX_WIDTH], axis=-1)
    q = q.reshape(B, S, FOX_HEADS, FOX_HEAD_DIM)
    k = k.reshape(B, S, FOX_HEADS, FOX_HEAD_DIM)
    v = v.reshape(B, S, FOX_HEADS, FOX_HEAD_DIM)
    log_f = jax.nn.log_sigmoid((f_logit + b_f).astype(jnp.float32))
    cum = jnp.cumsum(log_f, axis=1).transpose(0, 2, 1)
    scale = FOX_HEAD_DIM ** -0.5

    def block(start):
        qb = lax.dynamic_slice_in_dim(q, start, Q_BLOCK, axis=1)
        cq = lax.dynamic_slice_in_dim(cum, start, Q_BLOCK, axis=2)
        s = jnp.einsum('bqhd,bkhd->bhqk', qb, k).astype(jnp.float32) * scale
        s = s + (cq[..., :, None] - cum[..., None, :])
        p = masked_softmax(s, causal_mask(start, S))
        return jnp.einsum('bhqk,bkhd->bqhd', p.astype(v.dtype), v)

    o = causal_block_sweep(block, B, S)
    return o @ w_o


def apply_rope(x, cos, sin):
    half = x.shape[-1] // 2
    x1, x2 = x[..., :half], x[..., half:]
    return jnp.concatenate([x1 * cos - x2 * sin, x2 * cos + x1 * sin], axis=-1)


def latent_attention(h, w_a, g_q, g_kv, w_uq, w_ukv, w_o):
    B, S, _ = h.shape
    a = h @ w_a
    c_q, c_kv, k_rope = jnp.split(a, [MLA_Q_RANK, MLA_Q_RANK + MLA_KV_RANK], axis=-1)
    c_q = rms_norm(c_q) * g_q
    c_kv = rms_norm(c_kv) * g_kv
    q = (c_q @ w_uq).reshape(B, S, MLA_HEADS, MLA_NOPE_DIM + MLA_ROPE_DIM)
    q_nope, q_rope = q[..., :MLA_NOPE_DIM], q[..., MLA_NOPE_DIM:]
    kv = (c_kv @ w_ukv).reshape(B, S, MLA_HEADS, MLA_NOPE_DIM + MLA_V_DIM)
    k_nope, v = kv[..., :MLA_NOPE_DIM], kv[..., MLA_NOPE_DIM:]

    pos = jnp.arange(S, dtype=jnp.float32)
    inv_freq = ROPE_BASE ** (-jnp.arange(0, MLA_ROPE_DIM, 2, dtype=jnp.float32) / MLA_ROPE_DIM)
    ang = pos[:, None] * inv_freq[None, :]
    cos = jnp.cos(ang).astype(h.dtype)
    sin = jnp.sin(ang).astype(h.dtype)
    q_rope = apply_rope(q_rope, cos[:, None, :], sin[:, None, :])
    k_rope = apply_rope(k_rope, cos, sin)
    scale = (MLA_NOPE_DIM + MLA_ROPE_DIM) ** -0.5

    def block(start):
        qn = lax.dynamic_slice_in_dim(q_nope, start, Q_BLOCK, axis=1)
        qr = lax.dynamic_slice_in_dim(q_rope, start, Q_BLOCK, axis=1)
        s = jnp.einsum('bqhd,bkhd->bhqk', qn, k_nope) + jnp.einsum('bqhr,bkr->bhqk', qr, k_rope)
        p = masked_softmax(s.astype(jnp.float32) * scale, causal_mask(start, S))
        return jnp.einsum('bhqk,bkhd->bqhd', p.astype(v.dtype), v)

    o = causal_block_sweep(block, B, S)
    return o @ w_o


def conv_gated_mlp(h, w_in, conv_w, conv_b, w_out):
    S = h.shape[1]
    u = h @ w_in
    u_pad = jnp.pad(u, ((0, 0), (CONV_WIDTH - 1, 0), (0, 0)))
    y = conv_b
    for j in range(CONV_WIDTH):
        y = y + conv_w[j] * u_pad[:, j:j + S, :]
    gate, val = jnp.split(y, 2, axis=-1)
    return (jax.nn.silu(gate) * val) @ w_out


def setup_inputs(seed: int = 0) -> dict:
    key = jax.random.key(seed)
    ks = jax.random.split(key, 20)
    D = D_MODEL
    nrm = jax.random.normal
    f32 = jnp.float32
    x = nrm(ks[0], (BATCH, SEQ, D), f32)
    c = nrm(ks[1], (BATCH, D), f32)
    ada_w = nrm(ks[2], (DEPTH, 2, D, 3 * D), f32) * D ** -0.5
    ada_b = 0.02 * nrm(ks[3], (DEPTH, 2, 3 * D), f32)
    fox_w_in = nrm(ks[4], (N_FOX_LAYERS, D, 3 * FOX_WIDTH + FOX_HEADS), f32) * D ** -0.5
    fox_w_in = fox_w_in.at[..., 3 * FOX_WIDTH:].multiply(0.1)
    fox_b_f = jnp.linspace(2.0, 6.0, FOX_HEADS, dtype=f32)[None, :] + 0.1 * nrm(ks[5], (N_FOX_LAYERS, FOX_HEADS), f32)
    fox_w_o = nrm(ks[6], (N_FOX_LAYERS, FOX_WIDTH, D), f32) * FOX_WIDTH ** -0.5
    mla_w_a = nrm(ks[7], (N_MLA_LAYERS, D, MLA_Q_RANK + MLA_KV_RANK + MLA_ROPE_DIM), f32) * D ** -0.5
    mla_g_q = 1.0 + 0.02 * nrm(ks[8], (N_MLA_LAYERS, MLA_Q_RANK), f32)
    mla_g_kv = 1.0 + 0.02 * nrm(ks[9], (N_MLA_LAYERS, MLA_KV_RANK), f32)
    mla_w_uq = nrm(ks[10], (N_MLA_LAYERS, MLA_Q_RANK, MLA_HEADS * (MLA_NOPE_DIM + MLA_ROPE_DIM)), f32) * MLA_Q_RANK ** -0.5
    mla_w_ukv = nrm(ks[11], (N_MLA_LAYERS, MLA_KV_RANK, MLA_HEADS * (MLA_NOPE_DIM + MLA_V_DIM)), f32) * MLA_KV_RANK ** -0.5
    mla_w_o = nrm(ks[12], (N_MLA_LAYERS, MLA_HEADS * MLA_V_DIM, D), f32) * (MLA_HEADS * MLA_V_DIM) ** -0.5
    ffn_w_in = nrm(ks[13], (DEPTH, D, 2 * D_FF), f32) * D ** -0.5
    ffn_conv_w = nrm(ks[14], (DEPTH, CONV_WIDTH, 2 * D_FF), f32) * CONV_WIDTH ** -0.5
    ffn_conv_b = 0.02 * nrm(ks[15], (DEPTH, 2 * D_FF), f32)
    ffn_w_out = nrm(ks[16], (DEPTH, D_FF, D), f32) * D_FF ** -0.5
    final_g = 1.0 + 0.02 * nrm(ks[17], (D,), f32)
    return {'x': x, 'c': c, 'ada_w': ada_w, 'ada_b': ada_b,
            'fox_w_in': fox_w_in, 'fox_b_f': fox_b_f, 'fox_w_o': fox_w_o,
            'mla_w_a': mla_w_a, 'mla_g_q': mla_g_q, 'mla_g_kv': mla_g_kv,
            'mla_w_uq': mla_w_uq, 'mla_w_ukv': mla_w_ukv, 'mla_w_o': mla_w_o,
            'ffn_w_in': ffn_w_in, 'ffn_conv_w': ffn_conv_w, 'ffn_conv_b': ffn_conv_b,
            'ffn_w_out': ffn_w_out, 'final_g': final_g}


def reference(x, c, ada_w, ada_b, fox_w_in, fox_b_f, fox_w_o, mla_w_a, mla_g_q, mla_g_kv,
              mla_w_uq, mla_w_ukv, mla_w_o, ffn_w_in, ffn_conv_w, ffn_conv_b, ffn_w_out, final_g):
    for i in range(DEPTH):
        j = i // N_MIXERS
        shift, scale, gate = ada_modulation(c, ada_w[i, 0], ada_b[i, 0])
        h = rms_norm(x) * (1.0 + scale) + shift
        if i % N_MIXERS == 0:
            y = forgetting_attention(h, fox_w_in[j], fox_b_f[j], fox_w_o[j])
        else:
            y = latent_attention(h, mla_w_a[j], mla_g_q[j], mla_g_kv[j],
                                 mla_w_uq[j], mla_w_ukv[j], mla_w_o[j])
        x = x + gate * y
        shift, scale, gate = ada_modulation(c, ada_w[i, 1], ada_b[i, 1])
        h = rms_norm(x) * (1.0 + scale) + shift
        x = x + gate * conv_gated_mlp(h, ffn_w_in[i], ffn_conv_w[i], ffn_conv_b[i], ffn_w_out[i])
    return rms_norm(x) * final_g
```

```python
import functools

import jax
import jax.numpy as jnp
import numpy as np
from jax import lax
from jax.experimental import pallas as pl
from jax.experimental.pallas import tpu as pltpu

D_MODEL = 1024
N_HEADS = 16
HEAD_DIM = 64
N_PAIRS = N_HEADS // 2
LANES = 128
NORM_EPS = 1e-6
MLA_NOPE = 64
MLA_ROPE = 32
MLA_V = 64
MLA_Q_RANK = 384
MLA_KV_RANK = 256
ROPE_BASE = 10000.0
D_FF = 2816
CONV_WIDTH = 3

VMEM_LIMIT_BYTES = 56 * 1024 * 1024

ADA_TN = 1024
PROJ_TM = 512
ATTN_T = 512
FFN_TM = 1024
FFN_FC = 256
CARRY_ROWS = 8
NEG_BIG = -1e30


def _cparams(semantics):
    return pltpu.CompilerParams(dimension_semantics=semantics,
                                vmem_limit_bytes=VMEM_LIMIT_BYTES)


def _rms(x):
    return x * lax.rsqrt(jnp.mean(x * x, axis=-1, keepdims=True) + NORM_EPS)


def _mod_norm(x, mod_ref):
    shift = mod_ref[:, 0:D_MODEL]
    scale = mod_ref[:, D_MODEL:2 * D_MODEL]
    return _rms(x) * (1.0 + scale) + shift


def _dot(a, b):
    return jnp.dot(a, b, preferred_element_type=jnp.float32)


def _ada_kernel(c_ref, w_ref, b_ref, o_ref):
    c = c_ref[...]
    sc = (c * (1.0 / (1.0 + jnp.exp(-c)))).astype(jnp.bfloat16)
    o_ref[...] = _dot(sc, w_ref[...].astype(jnp.bfloat16)) + b_ref[...]


def _ada_modulation(c, ada_w, ada_b):
    n_mod = ada_w.shape[0] * ada_w.shape[1]
    batch = c.shape[0]
    rows = 8
    c_pad = jnp.zeros((rows, D_MODEL), jnp.float32).at[:batch].set(c)
    w = ada_w.reshape(n_mod, D_MODEL, 3 * D_MODEL)
    b = ada_b.reshape(n_mod, 1, 3 * D_MODEL)
    out = pl.pallas_call(
        _ada_kernel,
        grid=(n_mod, 3 * D_MODEL // ADA_TN),
        in_specs=[
            pl.BlockSpec((rows, D_MODEL), lambda l, n: (0, 0)),
            pl.BlockSpec((None, D_MODEL, ADA_TN), lambda l, n: (l, 0, n)),
            pl.BlockSpec((None, 1, ADA_TN), lambda l, n: (l, 0, n)),
        ],
        out_specs=pl.BlockSpec((None, rows, ADA_TN), lambda l, n: (l, 0, n)),
        out_shape=jax.ShapeDtypeStruct((n_mod, rows, 3 * D_MODEL), jnp.float32),
        compiler_params=_cparams(("arbitrary", "arbitrary")),
        name="ada_mod",
    )(c_pad, w, b)
    return out[:, :batch].reshape(n_mod, batch, 1, 3 * D_MODEL)


def _split3(x):
    hi = x.astype(jnp.bfloat16)
    r = x - hi.astype(jnp.float32)
    mid = r.astype(jnp.bfloat16)
    lo = (r - mid.astype(jnp.float32)).astype(jnp.bfloat16)
    return hi, mid, lo


def _fox_proj_kernel(x_ref, mod_ref, w_ref, wf_ref, bf_ref,
                     q_ref, k_ref, v_ref, cum_ref, carry_ref):
    i = pl.program_id(1)
    tm = x_ref.shape[0]
    h = _mod_norm(x_ref[...], mod_ref).astype(jnp.bfloat16)
    w = D_MODEL
    q_ref[...] = (_dot(h, w_ref[:, 0:w]) * (HEAD_DIM ** -0.5)).astype(q_ref.dtype)
    k_ref[...] = _dot(h, w_ref[:, w:2 * w]).astype(k_ref.dtype)
    v_ref[...] = _dot(h, w_ref[:, 2 * w:3 * w]).astype(v_ref.dtype)

    z = _dot(h, wf_ref[...]) + bf_ref[...]
    log_f = jnp.minimum(z, 0.0) - jnp.log(1.0 + jnp.exp(-jnp.abs(z)))

    @pl.when(i == 0)
    def _():
        carry_ref[...] = jnp.zeros_like(carry_ref)

    row = lax.broadcasted_iota(jnp.int32, (tm, tm), 0)
    col = lax.broadcasted_iota(jnp.int32, (tm, tm), 1)
    tri = (row >= col).astype(jnp.bfloat16)
    hi, mid, lo = _split3(log_f)
    cum = (_dot(tri, hi) + _dot(tri, mid)) + _dot(tri, lo) + carry_ref[0:1, :]
    cum_ref[...] = cum
    carry_ref[...] = jnp.broadcast_to(cum[tm - 1:tm, :], carry_ref.shape)


def _fox_projection(x, mod, w_in, b_f):
    batch, seq, _ = x.shape
    w_main = w_in[:, :3 * D_MODEL].astype(jnp.bfloat16)
    w_f = jnp.zeros((D_MODEL, LANES), jnp.float32).at[:, :N_HEADS].set(w_in[:, 3 * D_MODEL:])
    w_f = w_f.astype(jnp.bfloat16)
    bias = jnp.zeros((1, LANES), jnp.float32).at[0, :N_HEADS].set(b_f)
    tm = PROJ_TM
    row_spec = lambda width: pl.BlockSpec((None, tm, width), lambda b, i: (b, i, 0))
    full = lambda shape: pl.BlockSpec(shape, lambda b, i: tuple(0 for _ in shape))
    act = jax.ShapeDtypeStruct((batch, seq, D_MODEL), jnp.bfloat16)
    return pl.pallas_call(
        _fox_proj_kernel,
        grid=(batch, seq // tm),
        in_specs=[
            row_spec(D_MODEL),
            pl.BlockSpec((None, 1, 3 * D_MODEL), lambda b, i: (b, 0, 0)),
            full((D_MODEL, 3 * D_MODEL)),
            full((D_MODEL, LANES)),
            full((1, LANES)),
        ],
        out_specs=[row_spec(D_MODEL), row_spec(D_MODEL), row_spec(D_MODEL), row_spec(LANES)],
        out_shape=[act, act, act, jax.ShapeDtypeStruct((batch, seq, LANES), jnp.float32)],
        scratch_shapes=[pltpu.VMEM((CARRY_ROWS, LANES), jnp.float32)],
        compiler_params=_cparams(("arbitrary", "arbitrary")),
        name="fox_proj",
    )(x, mod, w_main, w_f, bias)


def _rope(x, cos, sin_signed):
    return x * cos + pltpu.roll(x, 32, axis=1) * sin_signed


def _mla_proj_kernel(x_ref, mod_ref, wa_ref, gq_ref, gkv_ref, wuq_ref, wkn_ref, wv_ref,
                     cos_ref, sin_ref, q_ref, kn_ref, kr_ref, v_ref):
    h = _mod_norm(x_ref[...], mod_ref).astype(jnp.bfloat16)
    a = _dot(h, wa_ref[...])
    kv0 = MLA_Q_RANK
    kr0 = MLA_Q_RANK + MLA_KV_RANK
    c_q = (_rms(a[:, 0:kv0]) * gq_ref[...]).astype(jnp.bfloat16)
    c_kv = (_rms(a[:, kv0:kr0]) * gkv_ref[...]).astype(jnp.bfloat16)
    cos = cos_ref[...]
    sin = sin_ref[...]
    kr_ref[...] = _rope(a[:, kr0:kr0 + LANES], cos, sin).astype(kr_ref.dtype)
    kn_ref[...] = _dot(c_kv, wkn_ref[...]).astype(kn_ref.dtype)
    v_ref[...] = _dot(c_kv, wv_ref[...]).astype(v_ref.dtype)
    scale = (MLA_NOPE + MLA_ROPE) ** -0.5
    for p in range(N_PAIRS):
        c0 = p * 2 * LANES
        qp = _dot(c_q, wuq_ref[:, c0:c0 + 2 * LANES])
        q_ref[:, c0:c0 + LANES] = (qp[:, 0:LANES] * scale).astype(q_ref.dtype)
        q_ref[:, c0 + LANES:c0 + 2 * LANES] = (
            _rope(qp[:, LANES:2 * LANES], cos, sin) * scale).astype(q_ref.dtype)


def _mla_weight_layout(w_a, w_uq, w_ukv):
    half = MLA_ROPE // 2
    kr0 = MLA_Q_RANK + MLA_KV_RANK
    x1 = np.arange(half)
    x2 = half + np.arange(half)
    kr_cols = kr0 + np.concatenate([x1, x1, x2, x2] * 2)
    wa_cols = np.concatenate([np.arange(kr0), kr_cols])
    q_cols = []
    qd = MLA_NOPE + MLA_ROPE
    for p in range(N_PAIRS):
        h0, h1 = 2 * p * qd, (2 * p + 1) * qd
        rope = np.concatenate([h0 + MLA_NOPE + x1, h1 + MLA_NOPE + x1,
                               h0 + MLA_NOPE + x2, h1 + MLA_NOPE + x2])
        q_cols.append(np.concatenate([h0 + np.arange(MLA_NOPE), h1 + np.arange(MLA_NOPE),
                                      rope, rope]))
    q_cols = np.concatenate(q_cols)
    kvd = MLA_NOPE + MLA_V
    kn_cols = np.concatenate([hh * kvd + np.arange(MLA_NOPE) for hh in range(N_HEADS)])
    v_cols = np.concatenate([hh * kvd + MLA_NOPE + np.arange(MLA_V) for hh in range(N_HEADS)])
    bf = jnp.bfloat16
    return (w_a[:, wa_cols].astype(bf), w_uq[:, q_cols].astype(bf),
            w_ukv[:, kn_cols].astype(bf), w_ukv[:, v_cols].astype(bf))


def _rope_tables(seq):
    half = MLA_ROPE // 2
    pos = jnp.arange(seq, dtype=jnp.float32)
    inv_freq = ROPE_BASE ** (-jnp.arange(0, MLA_ROPE, 2, dtype=jnp.float32) / MLA_ROPE)
    ang = pos[:, None] * inv_freq[None, :]
    cos = jnp.tile(jnp.cos(ang), (1, LANES // half))
    sign = np.where((np.arange(LANES) % (4 * half)) < 2 * half, -1.0, 1.0).astype(np.float32)
    sin = jnp.tile(jnp.sin(ang), (1, LANES // half)) * sign[None, :]
    return cos, sin


def _mla_projection(x, mod, w_a, g_q, g_kv, w_uq, w_ukv):
    batch, seq, _ = x.shape
    wa, wuq, wkn, wv = _mla_weight_layout(w_a, w_uq, w_ukv)
    cos, sin = _rope_tables(seq)
    tm = PROJ_TM
    row_spec = lambda width: pl.BlockSpec((None, tm, width), lambda b, i: (b, i, 0))
    full = lambda shape: pl.BlockSpec(shape, lambda b, i: tuple(0 for _ in shape))
    tab = pl.BlockSpec((tm, LANES), lambda b, i: (i, 0))
    bf = jnp.bfloat16
    return pl.pallas_call(
        _mla_proj_kernel,
        grid=(batch, seq // tm),
        in_specs=[
            row_spec(D_MODEL),
            pl.BlockSpec((None, 1, 3 * D_MODEL), lambda b, i: (b, 0, 0)),
            full(wa.shape), full((1, MLA_Q_RANK)), full((1, MLA_KV_RANK)),
            full(wuq.shape), full(wkn.shape), full(wv.shape), tab, tab,
        ],
        out_specs=[row_spec(2 * D_MODEL), row_spec(D_MODEL), row_spec(LANES), row_spec(D_MODEL)],
        out_shape=[jax.ShapeDtypeStruct((batch, seq, 2 * D_MODEL), bf),
                   jax.ShapeDtypeStruct((batch, seq, D_MODEL), bf),
                   jax.ShapeDtypeStruct((batch, seq, LANES), bf),
                   jax.ShapeDtypeStruct((batch, seq, D_MODEL), bf)],
        compiler_params=_cparams(("arbitrary", "arbitrary")),
        name="mla_proj",
    )(x, mod, wa, g_q.reshape(1, -1), g_kv.reshape(1, -1), wuq, wkn, wv, cos, sin)


def _head_lane_masks(width):
    lane = lax.broadcasted_iota(jnp.int32, (1, width), 1)
    m0 = lane < HEAD_DIM
    m1 = (lane >= HEAD_DIM) & (lane < 2 * HEAD_DIM)
    if width > LANES:
        r = lane - LANES
        in_rope = (r >= 0) & (r < 2 * MLA_ROPE)
        grp = (r >> 4) & 1
        m0 = m0 | (in_rope & (grp == 0))
        m1 = m1 | (in_rope & (grp == 1))
    return m0, m1


def _attn_kernel(*refs, decay, has_rope):
    if decay:
        q_ref, k_ref, v_ref, cq_ref, ck_ref, o_ref, m_ref, l_ref, acc_ref = refs
    elif has_rope:
        q_ref, k_ref, kr_ref, v_ref, o_ref, m_ref, l_ref, acc_ref = refs
    else:
        q_ref, k_ref, v_ref, o_ref, m_ref, l_ref, acc_ref = refs
    pair = pl.program_id(1)
    qi = pl.program_id(2)
    t = q_ref.shape[0]
    width = q_ref.shape[1]

    q = q_ref[...]
    masks = _head_lane_masks(width)
    q_heads = [jnp.where(m, q, jnp.zeros_like(q)) for m in masks]
    if decay:
        lane = lax.broadcasted_iota(jnp.int32, (1, LANES), 1)
        cq_all = cq_ref[...]
        cq_cols = [jnp.sum(jnp.where(lane == 2 * pair + hh, cq_all, 0.0), axis=1, keepdims=True)
                   for hh in range(2)]

    m_ref[...] = jnp.full(m_ref.shape, -jnp.inf, jnp.float32)
    l_ref[...] = jnp.zeros_like(l_ref)
    acc_ref[...] = jnp.zeros_like(acc_ref)

    def step(j, diagonal):
        start = pl.multiple_of(j * t, t)
        k_t = k_ref[pl.ds(start, t), :]
        if has_rope:
            k_t = jnp.concatenate([k_t, kr_ref[pl.ds(start, t), :]], axis=1)
        v_t = v_ref[pl.ds(start, t), :]
        if diagonal:
            row = lax.broadcasted_iota(jnp.int32, (t, t), 0)
            col = lax.broadcasted_iota(jnp.int32, (t, t), 1)
            keep = row >= col
        for hh in range(2):
            s = lax.dot_general(q_heads[hh], k_t, (((1,), (1,)), ((), ())),
                                preferred_element_type=jnp.float32)
            if decay:
                s = s + (cq_cols[hh] - ck_ref[j, hh:hh + 1, :])
            if diagonal:
                s = jnp.where(keep, s, -jnp.inf)
            m_old = m_ref[hh]
            m_new = jnp.maximum(m_old, jnp.max(s, axis=1, keepdims=True))
            alpha = jnp.exp(m_old - m_new)
            p = jnp.exp(s - m_new)
            l_ref[hh] = alpha * l_ref[hh] + jnp.sum(p, axis=1, keepdims=True)
            acc_ref[hh] = alpha * acc_ref[hh] + _dot(p.astype(v_t.dtype), v_t)
            m_ref[hh] = m_new

    def body(j, carry):
        step(j, False)
        return carry

    lax.fori_loop(0, qi, body, 0)
    step(qi, True)

    out0 = acc_ref[0] / l_ref[0]
    out1 = acc_ref[1] / l_ref[1]
    lane_o = lax.broadcasted_iota(jnp.int32, (1, LANES), 1)
    o_ref[...] = jnp.where(lane_o < HEAD_DIM, out0, out1).astype(o_ref.dtype)


def _attention(q, k, v, *, cum=None, k_rope=None):
    batch, seq, _ = k.shape
    t = ATTN_T
    nt = seq // t
    decay = cum is not None
    has_rope = k_rope is not None
    width = q.shape[2] // N_PAIRS
    q_spec = pl.BlockSpec((None, t, width), lambda b, p, i: (b, i, p))
    kv_spec = pl.BlockSpec((None, seq, LANES), lambda b, p, i: (b, 0, p))
    in_specs = [q_spec, kv_spec]
    args = [q, k]
    if has_rope:
        in_specs.append(pl.BlockSpec((None, seq, LANES), lambda b, p, i: (b, 0, 0)))
        args.append(k_rope)
    in_specs.append(kv_spec)
    args.append(v)
    if decay:
        cum_k = cum[:, :, :N_HEADS].reshape(batch, nt, t, N_PAIRS, 2).transpose(0, 3, 1, 4, 2)
        in_specs.append(pl.BlockSpec((None, t, LANES), lambda b, p, i: (b, i, 0)))
        in_specs.append(pl.BlockSpec((None, None, nt, 2, t), lambda b, p, i: (b, p, 0, 0, 0)))
        args += [cum, cum_k]
    return pl.pallas_call(
        functools.partial(_attn_kernel, decay=decay, has_rope=has_rope),
        grid=(batch, N_PAIRS, nt),
        in_specs=in_specs,
        out_specs=pl.BlockSpec((None, t, LANES), lambda b, p, i: (b, i, p)),
        out_shape=jax.ShapeDtypeStruct((batch, seq, D_MODEL), jnp.bfloat16),
        scratch_shapes=[pltpu.VMEM((2, t, 1), jnp.float32),
                        pltpu.VMEM((2, t, 1), jnp.float32),
                        pltpu.VMEM((2, t, LANES), jnp.float32)],
        compiler_params=_cparams(("arbitrary", "arbitrary", "arbitrary")),
        name="attn_fox" if decay else "attn_mla",
    )(*args)


def _ffn_kernel(x_ref, o_ref, moda_ref, modf_ref, wo_ref, wg_ref, wv_ref, cwg_ref, cwv_ref,
                cbg_ref, cbv_ref, wout_ref, fg_ref, out_ref,
                xnew_ref, h_ref, acc_ref, ug_ref, uv_ref, carry_ref, *, final_norm):
    i = pl.program_id(1)
    f = pl.program_id(2)
    nf = pl.num_programs(2)
    tm = x_ref.shape[0]

    @pl.when(f == 0)
    def _():
        gate_a = moda_ref[:, 2 * D_MODEL:3 * D_MODEL]
        xn = x_ref[...] + gate_a * _dot(o_ref[...], wo_ref[...])
        xnew_ref[...] = xn
        h_ref[...] = _mod_norm(xn, modf_ref).astype(h_ref.dtype)
        acc_ref[...] = jnp.zeros_like(acc_ref)

    h = h_ref[...]

    def conv(u_ref, w_ref, cw_ref, cb_ref, slot):
        prev = carry_ref[f, slot]
        u_ref[0:CARRY_ROWS, :] = jnp.where(i == 0, jnp.zeros_like(prev), prev)
        u_ref[CARRY_ROWS:CARRY_ROWS + tm, :] = _dot(h, w_ref[...])
        carry_ref[f, slot] = u_ref[tm:tm + CARRY_ROWS, :]
        y = cb_ref[...]
        for tap in range(CONV_WIDTH):
            off = CARRY_ROWS - (CONV_WIDTH - 1) + tap
            y = y + cw_ref[tap:tap + 1, :] * u_ref[off:off + tm, :]
        return y

    yg = conv(ug_ref, wg_ref, cwg_ref, cbg_ref, 0)
    yv = conv(uv_ref, wv_ref, cwv_ref, cbv_ref, 1)
    act = (yg * (1.0 / (1.0 + jnp.exp(-yg)))) * yv
    acc_ref[...] += _dot(act.astype(jnp.bfloat16), wout_ref[...])

    @pl.when(f == nf - 1)
    def _():
        gate_f = modf_ref[:, 2 * D_MODEL:3 * D_MODEL]
        y = xnew_ref[...] + gate_f * acc_ref[...]
        if final_norm:
            y = _rms(y) * fg_ref[...]
        out_ref[...] = y


def _mixer_out_and_ffn(x, o, mod_a, mod_f, w_o, w_in, conv_w, conv_b, w_out, final_g, final_norm):
    batch, seq, _ = x.shape
    tm, fc = FFN_TM, FFN_FC
    nf = D_FF // fc
    bf = jnp.bfloat16
    w_in_b = w_in.astype(bf)
    row = lambda: pl.BlockSpec((None, tm, D_MODEL), lambda b, i, f: (b, i, 0))
    mod_spec = lambda: pl.BlockSpec((None, 1, 3 * D_MODEL), lambda b, i, f: (b, 0, 0))
    gate_cols = lambda r: pl.BlockSpec((r, fc), lambda b, i, f: (0, f))
    val_cols = lambda r: pl.BlockSpec((r, fc), lambda b, i, f: (0, nf + f))
    conv_b2 = conv_b.reshape(1, -1)
    return pl.pallas_call(
        functools.partial(_ffn_kernel, final_norm=final_norm),
        grid=(batch, seq // tm, nf),
        in_specs=[
            row(), row(), mod_spec(), mod_spec(),
            pl.BlockSpec((D_MODEL, D_MODEL), lambda b, i, f: (0, 0)),
            gate_cols(D_MODEL), val_cols(D_MODEL),
            gate_cols(CONV_WIDTH), val_cols(CONV_WIDTH),
            gate_cols(1), val_cols(1),
            pl.BlockSpec((fc, D_MODEL), lambda b, i, f: (f, 0)),
            pl.BlockSpec((1, D_MODEL), lambda b, i, f: (0, 0)),
        ],
        out_specs=row(),
        out_shape=jax.ShapeDtypeStruct((batch, seq, D_MODEL), jnp.float32),
        scratch_shapes=[
            pltpu.VMEM((tm, D_MODEL), jnp.float32),
            pltpu.VMEM((tm, D_MODEL), bf),
            pltpu.VMEM((tm, D_MODEL), jnp.float32),
            pltpu.VMEM((tm + CARRY_ROWS, fc), jnp.float32),
            pltpu.VMEM((tm + CARRY_ROWS, fc), jnp.float32),
            pltpu.VMEM((nf, 2, CARRY_ROWS, fc), jnp.float32),
        ],
        compiler_params=_cparams(("arbitrary", "arbitrary", "arbitrary")),
        name="mixer_out_ffn",
    )(x, o, mod_a, mod_f, w_o.astype(bf), w_in_b, w_in_b, conv_w, conv_w, conv_b2, conv_b2,
      w_out.astype(bf), final_g.reshape(1, -1))


def kernel(x, c, ada_w, ada_b, fox_w_in, fox_b_f, fox_w_o, mla_w_a, mla_g_q, mla_g_kv, mla_w_uq,
           mla_w_ukv, mla_w_o, ffn_w_in, ffn_conv_w, ffn_conv_b, ffn_w_out, final_g):
    depth = ada_w.shape[0]
    mods = _ada_modulation(c, ada_w, ada_b)
    for layer in range(depth):
        j = layer // 2
        mod_a, mod_f = mods[2 * layer], mods[2 * layer + 1]
        if layer % 2 == 0:
            q, k, v, cum = _fox_projection(x, mod_a, fox_w_in[j], fox_b_f[j])
            o = _attention(q, k, v, cum=cum)
            w_o = fox_w_o[j]
        else:
            q, kn, kr, v = _mla_projection(x, mod_a, mla_w_a[j], mla_g_q[j], mla_g_kv[j],
                                           mla_w_uq[j], mla_w_ukv[j])
            o = _attention(q, kn, v, k_rope=kr)
            w_o = mla_w_o[j]
        x = _mixer_out_and_ffn(x, o, mod_a, mod_f, w_o, ffn_w_in[layer], ffn_conv_w[layer],
                               ffn_conv_b[layer], ffn_w_out[layer], final_g,
                               final_norm=(layer == depth - 1))
    return x
```

```python
import functools

import jax
import jax.numpy as jnp
import numpy as np
from jax import lax
from jax.experimental import pallas as pl
from jax.experimental.pallas import tpu as pltpu

D_MODEL = 1024
N_HEADS = 16
HEAD_DIM = 64
N_PAIRS = N_HEADS // 2
LANES = 128
NORM_EPS = 1e-6
MLA_NOPE = 64
MLA_ROPE = 32
MLA_V = 64
MLA_Q_RANK = 384
MLA_KV_RANK = 256
ROPE_BASE = 10000.0
D_FF = 2816
CONV_WIDTH = 3

VMEM_LIMIT_BYTES = 56 * 1024 * 1024

ADA_TN = 1024
PROJ_TM = 512
ATTN_T = 512
FFN_TM = 1024
FFN_FC = 256
CARRY_ROWS = 8
ATTN_CB = 256
EXT_LANES = 6
LOG2E = 1.4426950408889634


def _cparams(semantics, flags=None):
    return pltpu.CompilerParams(dimension_semantics=semantics,
                                vmem_limit_bytes=VMEM_LIMIT_BYTES, flags=flags)


def _rms(x):
    return x * lax.rsqrt(jnp.mean(x * x, axis=-1, keepdims=True) + NORM_EPS)


def _mod_norm(x, mod_ref):
    shift = mod_ref[:, 0:D_MODEL]
    scale = mod_ref[:, D_MODEL:2 * D_MODEL]
    return _rms(x) * (1.0 + scale) + shift


def _dot(a, b):
    return jnp.dot(a, b, preferred_element_type=jnp.float32)


def _dot_nt(a, b):
    return lax.dot_general(a, b, (((1,), (1,)), ((), ())), preferred_element_type=jnp.float32)


def _ada_kernel(c_ref, w_ref, b_ref, o_ref):
    c = c_ref[...]
    sc = (c * (1.0 / (1.0 + jnp.exp(-c)))).astype(jnp.bfloat16)
    o_ref[...] = _dot(sc, w_ref[...].astype(jnp.bfloat16)) + b_ref[...]


def _ada_modulation(c, ada_w, ada_b):
    n_mod = ada_w.shape[0] * ada_w.shape[1]
    batch = c.shape[0]
    rows = 8
    c_pad = jnp.zeros((rows, D_MODEL), jnp.float32).at[:batch].set(c)
    w = ada_w.reshape(n_mod, D_MODEL, 3 * D_MODEL)
    b = ada_b.reshape(n_mod, 1, 3 * D_MODEL)
    out = pl.pallas_call(
        _ada_kernel,
        grid=(n_mod, 3 * D_MODEL // ADA_TN),
        in_specs=[
            pl.BlockSpec((rows, D_MODEL), lambda l, n: (0, 0)),
            pl.BlockSpec((None, D_MODEL, ADA_TN), lambda l, n: (l, 0, n)),
            pl.BlockSpec((None, 1, ADA_TN), lambda l, n: (l, 0, n)),
        ],
        out_specs=pl.BlockSpec((None, rows, ADA_TN), lambda l, n: (l, 0, n)),
        out_shape=jax.ShapeDtypeStruct((n_mod, rows, 3 * D_MODEL), jnp.float32),
        compiler_params=_cparams(("arbitrary", "arbitrary")),
        name="ada_mod",
    )(c_pad, w, b)
    return out[:, :batch].reshape(n_mod, batch, 1, 3 * D_MODEL)


def _split3(x):
    hi = x.astype(jnp.bfloat16).astype(jnp.float32)
    r = x - hi
    mid = r.astype(jnp.bfloat16).astype(jnp.float32)
    lo = (r - mid).astype(jnp.bfloat16).astype(jnp.float32)
    return hi, mid, lo


def _fox_proj_kernel(x_ref, mod_ref, wq_ref, wk_ref, wvt_ref, wf_ref, bf_ref, code_ref,
                     q_ref, k_ref, vt_ref, qx_ref, kx_ref, carry_ref):
    i = pl.program_id(1)
    tm = x_ref.shape[0]
    h = _mod_norm(x_ref[...], mod_ref).astype(jnp.bfloat16)
    q_ref[...] = (_dot(h, wq_ref[...]) * (HEAD_DIM ** -0.5 * LOG2E)).astype(q_ref.dtype)
    k_ref[...] = _dot(h, wk_ref[...]).astype(k_ref.dtype)
    vt_ref[...] = _dot_nt(wvt_ref[...], h).astype(vt_ref.dtype)

    z = _dot(h, wf_ref[...]) + bf_ref[...]
    log_f = jnp.minimum(z, 0.0) - jnp.log(1.0 + jnp.exp(-jnp.abs(z)))

    @pl.when(i == 0)
    def _():
        carry_ref[...] = jnp.zeros_like(carry_ref)

    row = lax.broadcasted_iota(jnp.int32, (tm, tm), 0)
    col = lax.broadcasted_iota(jnp.int32, (tm, tm), 1)
    tri = (row >= col).astype(jnp.bfloat16)
    bf = jnp.bfloat16
    hi, mid, lo = _split3(log_f)
    cum = (_dot(tri, hi.astype(bf)) + _dot(tri, mid.astype(bf))) + _dot(tri, lo.astype(bf))
    cum = cum + carry_ref[0:1, :]
    carry_ref[...] = jnp.broadcast_to(cum[tm - 1:tm, :], carry_ref.shape)

    hi, mid, lo = _split3(cum * LOG2E)
    code = code_ref[...]
    one = jnp.ones_like(hi)
    zero = jnp.zeros_like(hi)
    qx = jnp.where(code == 0, hi, jnp.where(code == 1, mid, jnp.where(code == 2, lo,
                   jnp.where(code < EXT_LANES, one, zero))))
    kx = jnp.where(code < 3, one, jnp.where(code == 3, -hi, jnp.where(code == 4, -mid,
                   jnp.where(code == 5, -lo, zero))))
    qx_ref[...] = qx.astype(qx_ref.dtype)
    kx_ref[...] = kx.astype(kx_ref.dtype)


def _fox_projection(x, mod, w_in, b_f):
    batch, seq, _ = x.shape
    bf = jnp.bfloat16
    d = D_MODEL
    wq = w_in[:, 0:d].astype(bf)
    wk = w_in[:, d:2 * d].astype(bf)
    wvt = w_in[:, 2 * d:3 * d].T.astype(bf)
    lane = np.arange(LANES)
    used = lane < EXT_LANES * N_HEADS
    head_of_lane = np.where(used, lane // EXT_LANES, 0)
    code = np.where(used, lane % EXT_LANES, EXT_LANES).astype(np.int32)[None, :]
    w_f = jnp.where(used[None, :], w_in[:, 3 * d:][:, head_of_lane], 0.0).astype(bf)
    bias = jnp.where(used, b_f[head_of_lane], 0.0)[None, :]
    tm = PROJ_TM
    row_spec = lambda width: pl.BlockSpec((None, tm, width), lambda b, i: (b, i, 0))
    full = lambda shape: pl.BlockSpec(shape, lambda b, i: tuple(0 for _ in shape))
    act = jax.ShapeDtypeStruct((batch, seq, d), bf)
    ext = jax.ShapeDtypeStruct((batch, seq, LANES), bf)
    return pl.pallas_call(
        _fox_proj_kernel,
        grid=(batch, seq // tm),
        in_specs=[
            row_spec(d),
            pl.BlockSpec((None, 1, 3 * d), lambda b, i: (b, 0, 0)),
            full((d, d)), full((d, d)), full((d, d)),
            full((d, LANES)), full((1, LANES)), full((1, LANES)),
        ],
        out_specs=[row_spec(d), row_spec(d),
                   pl.BlockSpec((None, None, d, tm), lambda b, i: (b, i, 0, 0)),
                   row_spec(LANES), row_spec(LANES)],
        out_shape=[act, act, jax.ShapeDtypeStruct((batch, seq // tm, d, tm), bf), ext, ext],
        scratch_shapes=[pltpu.VMEM((CARRY_ROWS, LANES), jnp.float32)],
        compiler_params=_cparams(("arbitrary", "arbitrary")),
        name="fox_proj",
    )(x, mod, wq, wk, wvt, w_f, bias, jnp.asarray(code))


def _rope(x, cos, sin_signed):
    return x * cos + pltpu.roll(x, 32, axis=1) * sin_signed


def _mla_proj_kernel(x_ref, mod_ref, wa_ref, gq_ref, gkv_ref, wuq_ref, wkn_ref, wvt_ref,
                     cos_ref, sin_ref, q_ref, kn_ref, kr_ref, vt_ref):
    h = _mod_norm(x_ref[...], mod_ref).astype(jnp.bfloat16)
    a = _dot(h, wa_ref[...])
    kv0 = MLA_Q_RANK
    kr0 = MLA_Q_RANK + MLA_KV_RANK
    c_q = (_rms(a[:, 0:kv0]) * gq_ref[...]).astype(jnp.bfloat16)
    c_kv = (_rms(a[:, kv0:kr0]) * gkv_ref[...]).astype(jnp.bfloat16)
    cos = cos_ref[...]
    sin = sin_ref[...]
    kr_ref[...] = _rope(a[:, kr0:kr0 + LANES], cos, sin).astype(kr_ref.dtype)
    kn_ref[...] = _dot(c_kv, wkn_ref[...]).astype(kn_ref.dtype)
    vt_ref[...] = _dot_nt(wvt_ref[...], c_kv).astype(vt_ref.dtype)
    scale = (MLA_NOPE + MLA_ROPE) ** -0.5 * LOG2E
    for p in range(N_PAIRS):
        c0 = p * 2 * LANES
        qp = _dot(c_q, wuq_ref[:, c0:c0 + 2 * LANES])
        q_ref[:, c0:c0 + LANES] = (qp[:, 0:LANES] * scale).astype(q_ref.dtype)
        q_ref[:, c0 + LANES:c0 + 2 * LANES] = (
            _rope(qp[:, LANES:2 * LANES], cos, sin) * scale).astype(q_ref.dtype)


def _mla_weight_layout(w_a, w_uq, w_ukv):
    half = MLA_ROPE // 2
    kr0 = MLA_Q_RANK + MLA_KV_RANK
    x1 = np.arange(half)
    x2 = half + np.arange(half)
    kr_cols = kr0 + np.concatenate([x1, x1, x2, x2] * 2)
    wa_cols = np.concatenate([np.arange(kr0), kr_cols])
    q_cols = []
    qd = MLA_NOPE + MLA_ROPE
    for p in range(N_PAIRS):
        h0, h1 = 2 * p * qd, (2 * p + 1) * qd
        rope = np.concatenate([h0 + MLA_NOPE + x1, h1 + MLA_NOPE + x1,
                               h0 + MLA_NOPE + x2, h1 + MLA_NOPE + x2])
        q_cols.append(np.concatenate([h0 + np.arange(MLA_NOPE), h1 + np.arange(MLA_NOPE),
                                      rope, rope]))
    q_cols = np.concatenate(q_cols)
    kvd = MLA_NOPE + MLA_V
    kn_cols = np.concatenate([hh * kvd + np.arange(MLA_NOPE) for hh in range(N_HEADS)])
    v_cols = np.concatenate([hh * kvd + MLA_NOPE + np.arange(MLA_V) for hh in range(N_HEADS)])
    bf = jnp.bfloat16
    return (w_a[:, wa_cols].astype(bf), w_uq[:, q_cols].astype(bf),
            w_ukv[:, kn_cols].astype(bf), w_ukv[:, v_cols].T.astype(bf))


def _rope_tables(seq):
    half = MLA_ROPE // 2
    pos = jnp.arange(seq, dtype=jnp.float32)
    inv_freq = ROPE_BASE ** (-jnp.arange(0, MLA_ROPE, 2, dtype=jnp.float32) / MLA_ROPE)
    ang = pos[:, None] * inv_freq[None, :]
    cos = jnp.tile(jnp.cos(ang), (1, LANES // half))
    sign = np.where((np.arange(LANES) % (4 * half)) < 2 * half, -1.0, 1.0).astype(np.float32)
    sin = jnp.tile(jnp.sin(ang), (1, LANES // half)) * sign[None, :]
    return cos, sin


def _mla_projection(x, mod, w_a, g_q, g_kv, w_uq, w_ukv):
    batch, seq, _ = x.shape
    wa, wuq, wkn, wvt = _mla_weight_layout(w_a, w_uq, w_ukv)
    cos, sin = _rope_tables(seq)
    tm = PROJ_TM
    d = D_MODEL
    row_spec = lambda width: pl.BlockSpec((None, tm, width), lambda b, i: (b, i, 0))
    full = lambda shape: pl.BlockSpec(shape, lambda b, i: tuple(0 for _ in shape))
    tab = pl.BlockSpec((tm, LANES), lambda b, i: (i, 0))
    bf = jnp.bfloat16
    return pl.pallas_call(
        _mla_proj_kernel,
        grid=(batch, seq // tm),
        in_specs=[
            row_spec(d),
            pl.BlockSpec((None, 1, 3 * d), lambda b, i: (b, 0, 0)),
            full(wa.shape), full((1, MLA_Q_RANK)), full((1, MLA_KV_RANK)),
            full(wuq.shape), full(wkn.shape), full(wvt.shape), tab, tab,
        ],
        out_specs=[row_spec(2 * d), row_spec(d), row_spec(LANES),
                   pl.BlockSpec((None, None, d, tm), lambda b, i: (b, i, 0, 0))],
        out_shape=[jax.ShapeDtypeStruct((batch, seq, 2 * d), bf),
                   jax.ShapeDtypeStruct((batch, seq, d), bf),
                   jax.ShapeDtypeStruct((batch, seq, LANES), bf),
                   jax.ShapeDtypeStruct((batch, seq // tm, d, tm), bf)],
        compiler_params=_cparams(("arbitrary", "arbitrary")),
        name="mla_proj",
    )(x, mod, wa, g_q.reshape(1, -1), g_kv.reshape(1, -1), wuq, wkn, wvt, cos, sin)


def _head_lane_masks(pair, fox):
    lane = lax.broadcasted_iota(jnp.int32, (1, 2 * LANES), 1)
    masks = []
    for hh in range(2):
        m = (lane >= hh * HEAD_DIM) & (lane < (hh + 1) * HEAD_DIM)
        r = lane - LANES
        if fox:
            lo = (2 * pair + hh) * EXT_LANES
            m = m | ((r >= lo) & (r < lo + EXT_LANES))
        else:
            in_rope = (r >= 0) & (r < 2 * MLA_ROPE)
            m = m | (in_rope & (((r >> 4) & 1) == hh))
        masks.append(m)
    return masks


def _attn_kernel(*refs, fox):
    if fox:
        q_ref, qx_ref, k_ref, kx_ref, vt_ref, o_ref, m_ref, acc_ref = refs
    else:
        q_ref, k_ref, kx_ref, vt_ref, o_ref, m_ref, acc_ref = refs
    pair = pl.program_id(1)
    qi = pl.program_id(2)
    t = ATTN_T
    cb = ATTN_CB

    q = q_ref[...]
    if fox:
        q = jnp.concatenate([q, qx_ref[...]], axis=1)
    q_heads = [jnp.where(m, q, jnp.zeros_like(q)) for m in _head_lane_masks(pair, fox)]

    m_ref[...] = jnp.full(m_ref.shape, -jnp.inf, jnp.float32)
    acc_ref[...] = jnp.zeros_like(acc_ref)
    vrow = lax.broadcasted_iota(jnp.int32, (LANES, 1), 0)

    def step(j, diagonal):
        start = pl.multiple_of(j * t, t)
        k_t = jnp.concatenate([k_ref[pl.ds(start, t), :], kx_ref[pl.ds(start, t), :]], axis=1)
        vt = vt_ref[j]
        ones = jnp.ones_like(vt)
        vts = [jnp.where(vrow < HEAD_DIM, vt, ones), jnp.where(vrow >= HEAD_DIM, vt, ones)]
        chains = [(hh, c) for hh in range(2) for c in range(t // cb)]

        def scores(chain):
            hh, c = chain
            s = _dot_nt(k_t, q_heads[hh][c * cb:(c + 1) * cb, :])
            if diagonal:
                key = lax.broadcasted_iota(jnp.int32, (t, cb), 0)
                qry = lax.broadcasted_iota(jnp.int32, (t, cb), 1) + c * cb
                s = jnp.where(qry >= key, s, -jnp.inf)
            return s

        def update(chain, s):
            hh, c = chain
            cols = slice(c * cb, (c + 1) * cb)
            m_old = m_ref[hh, :, cols]
            m_new = jnp.maximum(m_old, jnp.max(s, axis=0, keepdims=True))
            alpha = jnp.exp2(m_old - m_new)
            p = jnp.exp2(s - m_new).astype(jnp.bfloat16)
            acc_ref[hh, :, cols] = alpha * acc_ref[hh, :, cols] + _dot(vts[hh], p)
            m_ref[hh, :, cols] = m_new

        s_next = scores(chains[0])
        for n, chain in enumerate(chains):
            s_cur = s_next
            if n + 1 < len(chains):
                s_next = scores(chains[n + 1])
            update(chain, s_cur)

    def body(j, carry):
        step(j, False)
        return carry

    lax.fori_loop(0, qi, body, 0)
    step(qi, True)

    a0 = acc_ref[0]
    a1 = acc_ref[1]
    first = vrow < HEAD_DIM
    num = jnp.where(first, a0, a1)
    den = jnp.where(first, a0[HEAD_DIM:HEAD_DIM + 1, :], a1[0:1, :])
    o_ref[...] = (num / den).T.astype(o_ref.dtype)


def _attention(q, k, kx, vt, *, qx=None):
    batch, seq, _ = k.shape
    t = ATTN_T
    nt = seq // t
    fox = qx is not None
    width = q.shape[2] // N_PAIRS
    shared_rows = lambda rows: pl.BlockSpec((None, rows, LANES), lambda b, p, i: (b, 0, 0))
    in_specs = [pl.BlockSpec((None, t, width), lambda b, p, i: (b, i, p))]
    args = [q]
    if fox:
        in_specs.append(pl.BlockSpec((None, t, LANES), lambda b, p, i: (b, i, 0)))
        args.append(qx)
    in_specs += [pl.BlockSpec((None, seq, LANES), lambda b, p, i: (b, 0, p)),
                 shared_rows(seq),
                 pl.BlockSpec((None, nt, LANES, t), lambda b, p, i: (b, 0, p, 0))]
    args += [k, kx, vt]
    return pl.pallas_call(
        functools.partial(_attn_kernel, fox=fox),
        grid=(batch, N_PAIRS, nt),
        in_specs=in_specs,
        out_specs=pl.BlockSpec((None, t, LANES), lambda b, p, i: (b, i, p)),
        out_shape=jax.ShapeDtypeStruct((batch, seq, D_MODEL), jnp.bfloat16),
        scratch_shapes=[pltpu.VMEM((2, 1, t), jnp.float32),
                        pltpu.VMEM((2, LANES, t), jnp.float32)],
        compiler_params=_cparams(("arbitrary", "arbitrary", "arbitrary")),
        name="attn_fox" if fox else "attn_mla",
    )(*args)


def _ffn_kernel(x_ref, o_ref, moda_ref, modf_ref, wo_ref, wg_ref, wv_ref, cwg_ref, cwv_ref,
                cbg_ref, cbv_ref, wout_ref, fg_ref, out_ref,
                xnew_ref, h_ref, acc_ref, ug_ref, uv_ref, carry_ref, *, final_norm):
    i = pl.program_id(1)
    f = pl.program_id(2)
    nf = pl.num_programs(2)
    tm = x_ref.shape[0]

    @pl.when(f == 0)
    def _():
        gate_a = moda_ref[:, 2 * D_MODEL:3 * D_MODEL]
        xn = x_ref[...] + gate_a * _dot(o_ref[...], wo_ref[...])
        xnew_ref[...] = xn
        h_ref[...] = _mod_norm(xn, modf_ref).astype(h_ref.dtype)
        acc_ref[...] = jnp.zeros_like(acc_ref)

    h = h_ref[...]

    def conv(u_ref, w_ref, cw_ref, cb_ref, slot):
        prev = carry_ref[f, slot]
        u_ref[0:CARRY_ROWS, :] = jnp.where(i == 0, jnp.zeros_like(prev), prev)
        u_ref[CARRY_ROWS:CARRY_ROWS + tm, :] = _dot(h, w_ref[...])
        carry_ref[f, slot] = u_ref[tm:tm + CARRY_ROWS, :]
        y = cb_ref[...]
        for tap in range(CONV_WIDTH):
            off = CARRY_ROWS - (CONV_WIDTH - 1) + tap
            y = y + cw_ref[tap:tap + 1, :] * u_ref[off:off + tm, :]
        return y

    yg = conv(ug_ref, wg_ref, cwg_ref, cbg_ref, 0)
    yv = conv(uv_ref, wv_ref, cwv_ref, cbv_ref, 1)
    act = (yg * (1.0 / (1.0 + jnp.exp(-yg)))) * yv
    acc_ref[...] += _dot(act.astype(jnp.bfloat16), wout_ref[...])

    @pl.when(f == nf - 1)
    def _():
        gate_f = modf_ref[:, 2 * D_MODEL:3 * D_MODEL]
        y = xnew_ref[...] + gate_f * acc_ref[...]
        if final_norm:
            y = _rms(y) * fg_ref[...]
        out_ref[...] = y


def _mixer_out_and_ffn(x, o, mod_a, mod_f, w_o, w_in, conv_w, conv_b, w_out, final_g, final_norm):
    batch, seq, _ = x.shape
    tm, fc = FFN_TM, FFN_FC
    nf = D_FF // fc
    bf = jnp.bfloat16
    w_in_b = w_in.astype(bf)
    row = lambda: pl.BlockSpec((None, tm, D_MODEL), lambda b, i, f: (b, i, 0))
    mod_spec = lambda: pl.BlockSpec((None, 1, 3 * D_MODEL), lambda b, i, f: (b, 0, 0))
    gate_cols = lambda r: pl.BlockSpec((r, fc), lambda b, i, f: (0, f))
    val_cols = lambda r: pl.BlockSpec((r, fc), lambda b, i, f: (0, nf + f))
    conv_b2 = conv_b.reshape(1, -1)
    return pl.pallas_call(
        functools.partial(_ffn_kernel, final_norm=final_norm),
        grid=(batch, seq // tm, nf),
        in_specs=[
            row(), row(), mod_spec(), mod_spec(),
            pl.BlockSpec((D_MODEL, D_MODEL), lambda b, i, f: (0, 0)),
            gate_cols(D_MODEL), val_cols(D_MODEL),
            gate_cols(CONV_WIDTH), val_cols(CONV_WIDTH),
            gate_cols(1), val_cols(1),
            pl.BlockSpec((fc, D_MODEL), lambda b, i, f: (f, 0)),
            pl.BlockSpec((1, D_MODEL), lambda b, i, f: (0, 0)),
        ],
        out_specs=row(),
        out_shape=jax.ShapeDtypeStruct((batch, seq, D_MODEL), jnp.float32),
        scratch_shapes=[
            pltpu.VMEM((tm, D_MODEL), jnp.float32),
            pltpu.VMEM((tm, D_MODEL), bf),
            pltpu.VMEM((tm, D_MODEL), jnp.float32),
            pltpu.VMEM((tm + CARRY_ROWS, fc), jnp.float32),
            pltpu.VMEM((tm + CARRY_ROWS, fc), jnp.float32),
            pltpu.VMEM((nf, 2, CARRY_ROWS, fc), jnp.float32),
        ],
        compiler_params=_cparams(("arbitrary", "arbitrary", "arbitrary")),
        name="mixer_out_ffn",
    )(x, o, mod_a, mod_f, w_o.astype(bf), w_in_b, w_in_b, conv_w, conv_w, conv_b2, conv_b2,
      w_out.astype(bf), final_g.reshape(1, -1))


def kernel(x, c, ada_w, ada_b, fox_w_in, fox_b_f, fox_w_o, mla_w_a, mla_g_q, mla_g_kv, mla_w_uq,
           mla_w_ukv, mla_w_o, ffn_w_in, ffn_conv_w, ffn_conv_b, ffn_w_out, final_g):
    depth = ada_w.shape[0]
    mods = _ada_modulation(c, ada_w, ada_b)
    for layer in range(depth):
        j = layer // 2
        mod_a, mod_f = mods[2 * layer], mods[2 * layer + 1]
        if layer % 2 == 0:
            q, k, vt, qx, kx = _fox_projection(x, mod_a, fox_w_in[j], fox_b_f[j])
            o = _attention(q, k, kx, vt, qx=qx)
            w_o = fox_w_o[j]
        else:
            q, kn, kr, vt = _mla_projection(x, mod_a, mla_w_a[j], mla_g_q[j], mla_g_kv[j],
                                            mla_w_uq[j], mla_w_ukv[j])
            o = _attention(q, kn, kr, vt)
            w_o = mla_w_o[j]
        x = _mixer_out_and_ffn(x, o, mod_a, mod_f, w_o, ffn_w_in[layer], ffn_conv_w[layer],
                               ffn_conv_b[layer], ffn_w_out[layer], final_g,
                               final_norm=(layer == depth - 1))
    return x
```

```python
import functools

import jax
import jax.numpy as jnp
import numpy as np
from jax import lax
from jax.experimental import pallas as pl
from jax.experimental.pallas import tpu as pltpu

D_MODEL = 1024
N_HEADS = 16
HEAD_DIM = 64
N_PAIRS = N_HEADS // 2
LANES = 128
NORM_EPS = 1e-6
MLA_NOPE = 64
MLA_ROPE = 32
MLA_V = 64
MLA_Q_RANK = 384
MLA_KV_RANK = 256
ROPE_BASE = 10000.0
D_FF = 2816
CONV_WIDTH = 3

VMEM_LIMIT_BYTES = 56 * 1024 * 1024

ADA_TN = 1024
PROJ_TM = 512
ATTN_T = 512
FFN_TM = 1024
FFN_FC = 256
CARRY_ROWS = 8
EXT_LANES = 6
LOG2E = 1.4426950408889634


def _cparams(semantics, flags=None):
    return pltpu.CompilerParams(dimension_semantics=semantics,
                                vmem_limit_bytes=VMEM_LIMIT_BYTES, flags=flags)


def _rms(x):
    return x * lax.rsqrt(jnp.mean(x * x, axis=-1, keepdims=True) + NORM_EPS)


def _mod_norm(x, mod_ref):
    shift = mod_ref[:, 0:D_MODEL]
    scale = mod_ref[:, D_MODEL:2 * D_MODEL]
    return _rms(x) * (1.0 + scale) + shift


def _dot(a, b):
    return jnp.dot(a, b, preferred_element_type=jnp.float32)


def _dot_nt(a, b):
    return lax.dot_general(a, b, (((1,), (1,)), ((), ())), preferred_element_type=jnp.float32)


def _ada_kernel(c_ref, w_ref, b_ref, o_ref):
    c = c_ref[...]
    sc = (c * (1.0 / (1.0 + jnp.exp(-c)))).astype(jnp.bfloat16)
    o_ref[...] = _dot(sc, w_ref[...].astype(jnp.bfloat16)) + b_ref[...]


def _ada_modulation(c, ada_w, ada_b):
    n_mod = ada_w.shape[0] * ada_w.shape[1]
    batch = c.shape[0]
    rows = 8
    c_pad = jnp.zeros((rows, D_MODEL), jnp.float32).at[:batch].set(c)
    w = ada_w.reshape(n_mod, D_MODEL, 3 * D_MODEL)
    b = ada_b.reshape(n_mod, 1, 3 * D_MODEL)
    out = pl.pallas_call(
        _ada_kernel,
        grid=(n_mod, 3 * D_MODEL // ADA_TN),
        in_specs=[
            pl.BlockSpec((rows, D_MODEL), lambda l, n: (0, 0)),
            pl.BlockSpec((None, D_MODEL, ADA_TN), lambda l, n: (l, 0, n)),
            pl.BlockSpec((None, 1, ADA_TN), lambda l, n: (l, 0, n)),
        ],
        out_specs=pl.BlockSpec((None, rows, ADA_TN), lambda l, n: (l, 0, n)),
        out_shape=jax.ShapeDtypeStruct((n_mod, rows, 3 * D_MODEL), jnp.float32),
        compiler_params=_cparams(("arbitrary", "arbitrary")),
        name="ada_mod",
    )(c_pad, w, b)
    return out[:, :batch].reshape(n_mod, batch, 1, 3 * D_MODEL)


def _split3(x):
    hi = x.astype(jnp.bfloat16).astype(jnp.float32)
    r = x - hi
    mid = r.astype(jnp.bfloat16).astype(jnp.float32)
    lo = (r - mid).astype(jnp.bfloat16).astype(jnp.float32)
    return hi, mid, lo


def _fox_proj_kernel(x_ref, mod_ref, wq_ref, wk_ref, wvt_ref, wf_ref, bf_ref, code_ref,
                     q_ref, k_ref, vt_ref, qx_ref, kx_ref, carry_ref):
    i = pl.program_id(1)
    tm = x_ref.shape[0]
    h = _mod_norm(x_ref[...], mod_ref).astype(jnp.bfloat16)
    q_ref[...] = (_dot(h, wq_ref[...]) * (HEAD_DIM ** -0.5 * LOG2E)).astype(q_ref.dtype)
    k_ref[...] = _dot(h, wk_ref[...]).astype(k_ref.dtype)
    vt_ref[...] = _dot_nt(wvt_ref[...], h).astype(vt_ref.dtype)

    z = _dot(h, wf_ref[...]) + bf_ref[...]
    log_f = jnp.minimum(z, 0.0) - jnp.log(1.0 + jnp.exp(-jnp.abs(z)))

    @pl.when(i == 0)
    def _():
        carry_ref[...] = jnp.zeros_like(carry_ref)

    row = lax.broadcasted_iota(jnp.int32, (tm, tm), 0)
    col = lax.broadcasted_iota(jnp.int32, (tm, tm), 1)
    tri = (row >= col).astype(jnp.bfloat16)
    bf = jnp.bfloat16
    hi, mid, lo = _split3(log_f)
    cum = (_dot(tri, hi.astype(bf)) + _dot(tri, mid.astype(bf))) + _dot(tri, lo.astype(bf))
    cum = cum + carry_ref[0:1, :]
    carry_ref[...] = jnp.broadcast_to(cum[tm - 1:tm, :], carry_ref.shape)

    hi, mid, lo = _split3(cum * LOG2E)
    code = code_ref[...]
    one = jnp.ones_like(hi)
    zero = jnp.zeros_like(hi)
    qx = jnp.where(code == 0, hi, jnp.where(code == 1, mid, jnp.where(code == 2, lo,
                   jnp.where(code < EXT_LANES, one, zero))))
    kx = jnp.where(code < 3, one, jnp.where(code == 3, -hi, jnp.where(code == 4, -mid,
                   jnp.where(code == 5, -lo, zero))))
    qx_ref[...] = qx.astype(qx_ref.dtype)
    kx_ref[...] = kx.astype(kx_ref.dtype)


def _fox_projection(x, mod, w_in, b_f):
    batch, seq, _ = x.shape
    bf = jnp.bfloat16
    d = D_MODEL
    wq = w_in[:, 0:d].astype(bf)
    wk = w_in[:, d:2 * d].astype(bf)
    wvt = w_in[:, 2 * d:3 * d].T.astype(bf)
    lane = np.arange(LANES)
    used = lane < EXT_LANES * N_HEADS
    head_of_lane = np.where(used, lane // EXT_LANES, 0)
    code = np.where(used, lane % EXT_LANES, EXT_LANES).astype(np.int32)[None, :]
    w_f = jnp.where(used[None, :], w_in[:, 3 * d:][:, head_of_lane], 0.0).astype(bf)
    bias = jnp.where(used, b_f[head_of_lane], 0.0)[None, :]
    tm = PROJ_TM
    row_spec = lambda width: pl.BlockSpec((None, tm, width), lambda b, i: (b, i, 0))
    full = lambda shape: pl.BlockSpec(shape, lambda b, i: tuple(0 for _ in shape))
    act = jax.ShapeDtypeStruct((batch, seq, d), bf)
    ext = jax.ShapeDtypeStruct((batch, seq, LANES), bf)
    return pl.pallas_call(
        _fox_proj_kernel,
        grid=(batch, seq // tm),
        in_specs=[
            row_spec(d),
            pl.BlockSpec((None, 1, 3 * d), lambda b, i: (b, 0, 0)),
            full((d, d)), full((d, d)), full((d, d)),
            full((d, LANES)), full((1, LANES)), full((1, LANES)),
        ],
        out_specs=[row_spec(d), row_spec(d),
                   pl.BlockSpec((None, None, d, tm), lambda b, i: (b, i, 0, 0)),
                   row_spec(LANES), row_spec(LANES)],
        out_shape=[act, act, jax.ShapeDtypeStruct((batch, seq // tm, d, tm), bf), ext, ext],
        scratch_shapes=[pltpu.VMEM((CARRY_ROWS, LANES), jnp.float32)],
        compiler_params=_cparams(("arbitrary", "arbitrary")),
        name="fox_proj",
    )(x, mod, wq, wk, wvt, w_f, bias, jnp.asarray(code))


def _rope(x, cos, sin_signed):
    return x * cos + pltpu.roll(x, 32, axis=1) * sin_signed


def _mla_proj_kernel(x_ref, mod_ref, wa_ref, gq_ref, gkv_ref, wuq_ref, wkn_ref, wvt_ref,
                     cos_ref, sin_ref, q_ref, kn_ref, kr_ref, vt_ref):
    h = _mod_norm(x_ref[...], mod_ref).astype(jnp.bfloat16)
    a = _dot(h, wa_ref[...])
    kv0 = MLA_Q_RANK
    kr0 = MLA_Q_RANK + MLA_KV_RANK
    c_q = (_rms(a[:, 0:kv0]) * gq_ref[...]).astype(jnp.bfloat16)
    c_kv = (_rms(a[:, kv0:kr0]) * gkv_ref[...]).astype(jnp.bfloat16)
    cos = cos_ref[...]
    sin = sin_ref[...]
    kr_ref[...] = _rope(a[:, kr0:kr0 + LANES], cos, sin).astype(kr_ref.dtype)
    kn_ref[...] = _dot(c_kv, wkn_ref[...]).astype(kn_ref.dtype)
    vt_ref[...] = _dot_nt(wvt_ref[...], c_kv).astype(vt_ref.dtype)
    scale = (MLA_NOPE + MLA_ROPE) ** -0.5 * LOG2E
    for p in range(N_PAIRS):
        c0 = p * 2 * LANES
        qp = _dot(c_q, wuq_ref[:, c0:c0 + 2 * LANES])
        q_ref[:, c0:c0 + LANES] = (qp[:, 0:LANES] * scale).astype(q_ref.dtype)
        q_ref[:, c0 + LANES:c0 + 2 * LANES] = (
            _rope(qp[:, LANES:2 * LANES], cos, sin) * scale).astype(q_ref.dtype)


def _mla_weight_layout(w_a, w_uq, w_ukv):
    half = MLA_ROPE // 2
    kr0 = MLA_Q_RANK + MLA_KV_RANK
    x1 = np.arange(half)
    x2 = half + np.arange(half)
    kr_cols = kr0 + np.concatenate([x1, x1, x2, x2] * 2)
    wa_cols = np.concatenate([np.arange(kr0), kr_cols])
    q_cols = []
    qd = MLA_NOPE + MLA_ROPE
    for p in range(N_PAIRS):
        h0, h1 = 2 * p * qd, (2 * p + 1) * qd
        rope = np.concatenate([h0 + MLA_NOPE + x1, h1 + MLA_NOPE + x1,
                               h0 + MLA_NOPE + x2, h1 + MLA_NOPE + x2])
        q_cols.append(np.concatenate([h0 + np.arange(MLA_NOPE), h1 + np.arange(MLA_NOPE),
                                      rope, rope]))
    q_cols = np.concatenate(q_cols)
    kvd = MLA_NOPE + MLA_V
    kn_cols = np.concatenate([hh * kvd + np.arange(MLA_NOPE) for hh in range(N_HEADS)])
    v_cols = np.concatenate([hh * kvd + MLA_NOPE + np.arange(MLA_V) for hh in range(N_HEADS)])
    bf = jnp.bfloat16
    return (w_a[:, wa_cols].astype(bf), w_uq[:, q_cols].astype(bf),
            w_ukv[:, kn_cols].astype(bf), w_ukv[:, v_cols].T.astype(bf))


def _rope_tables(seq):
    half = MLA_ROPE // 2
    pos = jnp.arange(seq, dtype=jnp.float32)
    inv_freq = ROPE_BASE ** (-jnp.arange(0, MLA_ROPE, 2, dtype=jnp.float32) / MLA_ROPE)
    ang = pos[:, None] * inv_freq[None, :]
    cos = jnp.tile(jnp.cos(ang), (1, LANES // half))
    sign = np.where((np.arange(LANES) % (4 * half)) < 2 * half, -1.0, 1.0).astype(np.float32)
    sin = jnp.tile(jnp.sin(ang), (1, LANES // half)) * sign[None, :]
    return cos, sin


def _mla_projection(x, mod, w_a, g_q, g_kv, w_uq, w_ukv):
    batch, seq, _ = x.shape
    wa, wuq, wkn, wvt = _mla_weight_layout(w_a, w_uq, w_ukv)
    cos, sin = _rope_tables(seq)
    tm = PROJ_TM
    d = D_MODEL
    row_spec = lambda width: pl.BlockSpec((None, tm, width), lambda b, i: (b, i, 0))
    full = lambda shape: pl.BlockSpec(shape, lambda b, i: tuple(0 for _ in shape))
    tab = pl.BlockSpec((tm, LANES), lambda b, i: (i, 0))
    bf = jnp.bfloat16
    return pl.pallas_call(
        _mla_proj_kernel,
        grid=(batch, seq // tm),
        in_specs=[
            row_spec(d),
            pl.BlockSpec((None, 1, 3 * d), lambda b, i: (b, 0, 0)),
            full(wa.shape), full((1, MLA_Q_RANK)), full((1, MLA_KV_RANK)),
            full(wuq.shape), full(wkn.shape), full(wvt.shape), tab, tab,
        ],
        out_specs=[row_spec(2 * d), row_spec(d), row_spec(LANES),
                   pl.BlockSpec((None, None, d, tm), lambda b, i: (b, i, 0, 0))],
        out_shape=[jax.ShapeDtypeStruct((batch, seq, 2 * d), bf),
                   jax.ShapeDtypeStruct((batch, seq, d), bf),
                   jax.ShapeDtypeStruct((batch, seq, LANES), bf),
                   jax.ShapeDtypeStruct((batch, seq // tm, d, tm), bf)],
        compiler_params=_cparams(("arbitrary", "arbitrary")),
        name="mla_proj",
    )(x, mod, wa, g_q.reshape(1, -1), g_kv.reshape(1, -1), wuq, wkn, wvt, cos, sin)


def _head_lane_masks(pair, fox):
    lane = lax.broadcasted_iota(jnp.int32, (1, 2 * LANES), 1)
    masks = []
    for hh in range(2):
        m = (lane >= hh * HEAD_DIM) & (lane < (hh + 1) * HEAD_DIM)
        r = lane - LANES
        if fox:
            lo = (2 * pair + hh) * EXT_LANES
            m = m | ((r >= lo) & (r < lo + EXT_LANES))
        else:
            in_rope = (r >= 0) & (r < 2 * MLA_ROPE)
            m = m | (in_rope & (((r >> 4) & 1) == hh))
        masks.append(m)
    return masks


def _attn_kernel(*refs, fox):
    if fox:
        q_ref, qx_ref, k_ref, kx_ref, vt_ref, o_ref, m_ref, acc_ref, s_ref, cmax_ref = refs
    else:
        q_ref, k_ref, kx_ref, vt_ref, o_ref, m_ref, acc_ref, s_ref, cmax_ref = refs
    pair = pl.program_id(1)
    qi = pl.program_id(2)
    t = ATTN_T

    q = q_ref[...]
    if fox:
        q = jnp.concatenate([q, qx_ref[...]], axis=1)
    q_heads = [jnp.where(m, q, jnp.zeros_like(q)) for m in _head_lane_masks(pair, fox)]

    m_ref[...] = jnp.full(m_ref.shape, -jnp.inf, jnp.float32)
    acc_ref[...] = jnp.zeros_like(acc_ref)
    vrow = lax.broadcasted_iota(jnp.int32, (LANES, 1), 0)

    def scores(j, slot, causal):
        start = pl.multiple_of(j * t, t)
        k_t = jnp.concatenate([k_ref[pl.ds(start, t), :], kx_ref[pl.ds(start, t), :]], axis=1)
        for hh in range(2):
            s = _dot_nt(k_t, q_heads[hh])
            if causal is not None:
                key = lax.broadcasted_iota(jnp.int32, (t, t), 0)
                qry = lax.broadcasted_iota(jnp.int32, (t, t), 1)
                s = jnp.where((qry >= key) | causal, s, -jnp.inf)
            s_ref[slot, hh] = s
            cmax_ref[slot, hh] = jnp.max(s, axis=0, keepdims=True)

    def probs(slot):
        out = []
        for hh in range(2):
            m_old = m_ref[hh]
            m_new = jnp.maximum(m_old, cmax_ref[slot, hh])
            m_ref[hh] = m_new
            alpha = jnp.exp2(m_old - m_new)
            out.append((alpha, jnp.exp2((s_ref[slot, hh] - m_new).astype(jnp.bfloat16))))
        return out

    def accumulate(j, ps):
        vt = vt_ref[j]
        ones = jnp.ones_like(vt)
        vts = [jnp.where(vrow < HEAD_DIM, vt, ones), jnp.where(vrow >= HEAD_DIM, vt, ones)]
        for hh in range(2):
            alpha, p = ps[hh]
            acc_ref[hh] = alpha * acc_ref[hh] + _dot(vts[hh], p)

    def update(j, slot):
        accumulate(j, probs(slot))

    def advance(j, slot, causal):
        ps = probs(slot)
        scores(j + 1, 1 - slot, causal)
        accumulate(j, ps)

    scores(0, 0, qi > 0)

    def body(i, carry):
        advance(2 * i, 0, None)
        advance(2 * i + 1, 1, None)
        return carry

    lax.fori_loop(0, lax.div(qi - 1, 2), body, 0)
    odd = lax.rem(qi, 2) == 1

    @pl.when(odd)
    def _():
        advance(qi - 1, 0, False)
        update(qi, 1)

    @pl.when(jnp.logical_not(odd) & (qi >= 2))
    def _():
        advance(qi - 2, 0, None)
        advance(qi - 1, 1, False)

    @pl.when(jnp.logical_not(odd))
    def _():
        update(qi, 0)

    a0 = acc_ref[0]
    a1 = acc_ref[1]
    first = vrow < HEAD_DIM
    num = jnp.where(first, a0, a1)
    den = jnp.where(first, a0[HEAD_DIM:HEAD_DIM + 1, :], a1[0:1, :])
    o_ref[...] = (num / den).T.astype(o_ref.dtype)


def _attention(q, k, kx, vt, *, qx=None):
    batch, seq, _ = k.shape
    t = ATTN_T
    nt = seq // t
    fox = qx is not None
    width = q.shape[2] // N_PAIRS
    shared_rows = lambda rows: pl.BlockSpec((None, rows, LANES), lambda b, p, i: (b, 0, 0))
    in_specs = [pl.BlockSpec((None, t, width), lambda b, p, i: (b, i, p))]
    args = [q]
    if fox:
        in_specs.append(pl.BlockSpec((None, t, LANES), lambda b, p, i: (b, i, 0)))
        args.append(qx)
    in_specs += [pl.BlockSpec((None, seq, LANES), lambda b, p, i: (b, 0, p)),
                 shared_rows(seq),
                 pl.BlockSpec((None, nt, LANES, t), lambda b, p, i: (b, 0, p, 0))]
    args += [k, kx, vt]
    return pl.pallas_call(
        functools.partial(_attn_kernel, fox=fox),
        grid=(batch, N_PAIRS, nt),
        in_specs=in_specs,
        out_specs=pl.BlockSpec((None, t, LANES), lambda b, p, i: (b, i, p)),
        out_shape=jax.ShapeDtypeStruct((batch, seq, D_MODEL), jnp.bfloat16),
        scratch_shapes=[pltpu.VMEM((2, 1, t), jnp.float32),
                        pltpu.VMEM((2, LANES, t), jnp.float32),
                        pltpu.VMEM((2, 2, t, t), jnp.float32),
                        pltpu.VMEM((2, 2, 1, t), jnp.float32)],
        compiler_params=_cparams(("arbitrary", "arbitrary", "arbitrary")),
        name="attn_fox" if fox else "attn_mla",
    )(*args)


def _ffn_kernel(x_ref, o_ref, moda_ref, modf_ref, wo_ref, wg_ref, wv_ref, cwg_ref, cwv_ref,
                cbg_ref, cbv_ref, wout_ref, fg_ref, out_ref,
                xnew_ref, h_ref, acc_ref, ug_ref, uv_ref, carry_ref, *, final_norm):
    i = pl.program_id(1)
    f = pl.program_id(2)
    nf = pl.num_programs(2)
    tm = x_ref.shape[0]

    @pl.when(f == 0)
    def _():
        gate_a = moda_ref[:, 2 * D_MODEL:3 * D_MODEL]
        xn = x_ref[...] + gate_a * _dot(o_ref[...], wo_ref[...])
        xnew_ref[...] = xn
        h_ref[...] = _mod_norm(xn, modf_ref).astype(h_ref.dtype)
        acc_ref[...] = jnp.zeros_like(acc_ref)

    h = h_ref[...]

    def conv(u_ref, w_ref, cw_ref, cb_ref, slot):
        prev = carry_ref[f, slot]
        u_ref[0:CARRY_ROWS, :] = jnp.where(i == 0, jnp.zeros_like(prev), prev)
        u_ref[CARRY_ROWS:CARRY_ROWS + tm, :] = _dot(h, w_ref[...])
        carry_ref[f, slot] = u_ref[tm:tm + CARRY_ROWS, :]
        y = cb_ref[...]
        for tap in range(CONV_WIDTH):
            off = CARRY_ROWS - (CONV_WIDTH - 1) + tap
            y = y + cw_ref[tap:tap + 1, :] * u_ref[off:off + tm, :]
        return y

    yg = conv(ug_ref, wg_ref, cwg_ref, cbg_ref, 0)
    yv = conv(uv_ref, wv_ref, cwv_ref, cbv_ref, 1)
    act = (yg * (1.0 / (1.0 + jnp.exp(-yg)))) * yv
    acc_ref[...] += _dot(act.astype(jnp.bfloat16), wout_ref[...])

    @pl.when(f == nf - 1)
    def _():
        gate_f = modf_ref[:, 2 * D_MODEL:3 * D_MODEL]
        y = xnew_ref[...] + gate_f * acc_ref[...]
        if final_norm:
            y = _rms(y) * fg_ref[...]
        out_ref[...] = y


def _mixer_out_and_ffn(x, o, mod_a, mod_f, w_o, w_in, conv_w, conv_b, w_out, final_g, final_norm):
    batch, seq, _ = x.shape
    tm, fc = FFN_TM, FFN_FC
    nf = D_FF // fc
    bf = jnp.bfloat16
    w_in_b = w_in.astype(bf)
    row = lambda: pl.BlockSpec((None, tm, D_MODEL), lambda b, i, f: (b, i, 0))
    mod_spec = lambda: pl.BlockSpec((None, 1, 3 * D_MODEL), lambda b, i, f: (b, 0, 0))
    gate_cols = lambda r: pl.BlockSpec((r, fc), lambda b, i, f: (0, f))
    val_cols = lambda r: pl.BlockSpec((r, fc), lambda b, i, f: (0, nf + f))
    conv_b2 = conv_b.reshape(1, -1)
    return pl.pallas_call(
        functools.partial(_ffn_kernel, final_norm=final_norm),
        grid=(batch, seq // tm, nf),
        in_specs=[
            row(), row(), mod_spec(), mod_spec(),
            pl.BlockSpec((D_MODEL, D_MODEL), lambda b, i, f: (0, 0)),
            gate_cols(D_MODEL), val_cols(D_MODEL),
            gate_cols(CONV_WIDTH), val_cols(CONV_WIDTH),
            gate_cols(1), val_cols(1),
            pl.BlockSpec((fc, D_MODEL), lambda b, i, f: (f, 0)),
            pl.BlockSpec((1, D_MODEL), lambda b, i, f: (0, 0)),
        ],
        out_specs=row(),
        out_shape=jax.ShapeDtypeStruct((batch, seq, D_MODEL), jnp.float32),
        scratch_shapes=[
            pltpu.VMEM((tm, D_MODEL), jnp.float32),
            pltpu.VMEM((tm, D_MODEL), bf),
            pltpu.VMEM((tm, D_MODEL), jnp.float32),
            pltpu.VMEM((tm + CARRY_ROWS, fc), jnp.float32),
            pltpu.VMEM((tm + CARRY_ROWS, fc), jnp.float32),
            pltpu.VMEM((nf, 2, CARRY_ROWS, fc), jnp.float32),
        ],
        compiler_params=_cparams(("arbitrary", "arbitrary", "arbitrary")),
        name="mixer_out_ffn",
    )(x, o, mod_a, mod_f, w_o.astype(bf), w_in_b, w_in_b, conv_w, conv_w, conv_b2, conv_b2,
      w_out.astype(bf), final_g.reshape(1, -1))


def kernel(x, c, ada_w, ada_b, fox_w_in, fox_b_f, fox_w_o, mla_w_a, mla_g_q, mla_g_kv, mla_w_uq,
           mla_w_ukv, mla_w_o, ffn_w_in, ffn_conv_w, ffn_conv_b, ffn_w_out, final_g):
    depth = ada_w.shape[0]
    mods = _ada_modulation(c, ada_w, ada_b)
    for layer in range(depth):
        j = layer // 2
        mod_a, mod_f = mods[2 * layer], mods[2 * layer + 1]
        if layer % 2 == 0:
            q, k, vt, qx, kx = _fox_projection(x, mod_a, fox_w_in[j], fox_b_f[j])
            o = _attention(q, k, kx, vt, qx=qx)
            w_o = fox_w_o[j]
        else:
            q, kn, kr, vt = _mla_projection(x, mod_a, mla_w_a[j], mla_g_q[j], mla_g_kv[j],
                                            mla_w_uq[j], mla_w_ukv[j])
            o = _attention(q, kn, kr, vt)
            w_o = mla_w_o[j]
        x = _mixer_out_and_ffn(x, o, mod_a, mod_f, w_o, ffn_w_in[layer], ffn_conv_w[layer],
                               ffn_conv_b[layer], ffn_w_out[layer], final_g,
                               final_norm=(layer == depth - 1))
    return x
```

```python
import functools

import jax
import jax.numpy as jnp
import numpy as np
from jax import lax
from jax.experimental import pallas as pl
from jax.experimental.pallas import tpu as pltpu

D_MODEL = 1024
N_HEADS = 16
HEAD_DIM = 64
N_PAIRS = N_HEADS // 2
LANES = 128
NORM_EPS = 1e-6
MLA_NOPE = 64
MLA_ROPE = 32
MLA_V = 64
MLA_Q_RANK = 384
MLA_KV_RANK = 256
ROPE_BASE = 10000.0
D_FF = 2816
CONV_WIDTH = 3

VMEM_LIMIT_BYTES = 56 * 1024 * 1024

ADA_TN = 1024
PROJ_TM = 512
ATTN_TK = 512
ATTN_TQ = 1024
FFN_TM = 1024
FFN_FC = 256
CARRY_ROWS = 8
EXT_LANES = 6
LOG2E = 1.4426950408889634


def _cparams(semantics, flags=None):
    return pltpu.CompilerParams(dimension_semantics=semantics,
                                vmem_limit_bytes=VMEM_LIMIT_BYTES, flags=flags)


def _rms(x):
    return x * lax.rsqrt(jnp.mean(x * x, axis=-1, keepdims=True) + NORM_EPS)


def _mod_norm(x, mod_ref):
    shift = mod_ref[:, 0:D_MODEL]
    scale = mod_ref[:, D_MODEL:2 * D_MODEL]
    return _rms(x) * (1.0 + scale) + shift


def _dot(a, b):
    return jnp.dot(a, b, preferred_element_type=jnp.float32)


def _dot_nt(a, b):
    return lax.dot_general(a, b, (((1,), (1,)), ((), ())), preferred_element_type=jnp.float32)


def _ada_kernel(c_ref, w_ref, b_ref, o_ref):
    c = c_ref[...]
    sc = (c * (1.0 / (1.0 + jnp.exp(-c)))).astype(jnp.bfloat16)
    o_ref[...] = _dot(sc, w_ref[...].astype(jnp.bfloat16)) + b_ref[...]


def _ada_modulation(c, ada_w, ada_b):
    n_mod = ada_w.shape[0] * ada_w.shape[1]
    batch = c.shape[0]
    rows = 8
    c_pad = jnp.zeros((rows, D_MODEL), jnp.float32).at[:batch].set(c)
    w = ada_w.reshape(n_mod, D_MODEL, 3 * D_MODEL)
    b = ada_b.reshape(n_mod, 1, 3 * D_MODEL)
    out = pl.pallas_call(
        _ada_kernel,
        grid=(n_mod, 3 * D_MODEL // ADA_TN),
        in_specs=[
            pl.BlockSpec((rows, D_MODEL), lambda l, n: (0, 0)),
            pl.BlockSpec((None, D_MODEL, ADA_TN), lambda l, n: (l, 0, n)),
            pl.BlockSpec((None, 1, ADA_TN), lambda l, n: (l, 0, n)),
        ],
        out_specs=pl.BlockSpec((None, rows, ADA_TN), lambda l, n: (l, 0, n)),
        out_shape=jax.ShapeDtypeStruct((n_mod, rows, 3 * D_MODEL), jnp.float32),
        compiler_params=_cparams(("arbitrary", "arbitrary")),
        name="ada_mod",
    )(c_pad, w, b)
    return out[:, :batch].reshape(n_mod, batch, 1, 3 * D_MODEL)


def _split3(x):
    hi = x.astype(jnp.bfloat16).astype(jnp.float32)
    r = x - hi
    mid = r.astype(jnp.bfloat16).astype(jnp.float32)
    lo = (r - mid).astype(jnp.bfloat16).astype(jnp.float32)
    return hi, mid, lo


def _fox_proj_kernel(x_ref, mod_ref, wqt_ref, wk_ref, wvt_ref, wf_ref, bf_ref, code_ref,
                     qt_ref, k_ref, vt_ref, qxt_ref, kx_ref, carry_ref):
    i = pl.program_id(1)
    tm = x_ref.shape[0]
    h = _mod_norm(x_ref[...], mod_ref).astype(jnp.bfloat16)
    qt_ref[...] = (_dot_nt(wqt_ref[...], h) * (HEAD_DIM ** -0.5 * LOG2E)).astype(qt_ref.dtype)
    k_ref[...] = _dot(h, wk_ref[...]).astype(k_ref.dtype)
    vt_ref[...] = _dot_nt(wvt_ref[...], h).astype(vt_ref.dtype)

    z = _dot(h, wf_ref[...]) + bf_ref[...]
    log_f = jnp.minimum(z, 0.0) - jnp.log(1.0 + jnp.exp(-jnp.abs(z)))

    @pl.when(i == 0)
    def _():
        carry_ref[...] = jnp.zeros_like(carry_ref)

    row = lax.broadcasted_iota(jnp.int32, (tm, tm), 0)
    col = lax.broadcasted_iota(jnp.int32, (tm, tm), 1)
    tri = (row >= col).astype(jnp.bfloat16)
    bf = jnp.bfloat16
    hi, mid, lo = _split3(log_f)
    cum = (_dot(tri, hi.astype(bf)) + _dot(tri, mid.astype(bf))) + _dot(tri, lo.astype(bf))
    cum = cum + carry_ref[0:1, :]
    carry_ref[...] = jnp.broadcast_to(cum[tm - 1:tm, :], carry_ref.shape)

    hi, mid, lo = _split3(cum * LOG2E)
    code = code_ref[...]
    one = jnp.ones_like(hi)
    zero = jnp.zeros_like(hi)
    qx = jnp.where(code == 0, hi, jnp.where(code == 1, mid, jnp.where(code == 2, lo,
                   jnp.where(code < EXT_LANES, one, zero))))
    kx = jnp.where(code < 3, one, jnp.where(code == 3, -hi, jnp.where(code == 4, -mid,
                   jnp.where(code == 5, -lo, zero))))
    qxt_ref[...] = qx.T.astype(qxt_ref.dtype)
    kx_ref[...] = kx.astype(kx_ref.dtype)


def _fox_projection(x, mod, w_in, b_f):
    batch, seq, _ = x.shape
    bf = jnp.bfloat16
    d = D_MODEL
    wqt = w_in[:, 0:d].T.astype(bf)
    wk = w_in[:, d:2 * d].astype(bf)
    wvt = w_in[:, 2 * d:3 * d].T.astype(bf)
    lane = np.arange(LANES)
    used = lane < EXT_LANES * N_HEADS
    head_of_lane = np.where(used, lane // EXT_LANES, 0)
    code = np.where(used, lane % EXT_LANES, EXT_LANES).astype(np.int32)[None, :]
    w_f = jnp.where(used[None, :], w_in[:, 3 * d:][:, head_of_lane], 0.0).astype(bf)
    bias = jnp.where(used, b_f[head_of_lane], 0.0)[None, :]
    tm = PROJ_TM
    row_spec = lambda width: pl.BlockSpec((None, tm, width), lambda b, i: (b, i, 0))
    col_spec = lambda rows: pl.BlockSpec((None, None, rows, tm), lambda b, i: (b, i, 0, 0))
    full = lambda shape: pl.BlockSpec(shape, lambda b, i: tuple(0 for _ in shape))
    rows_of = lambda width: jax.ShapeDtypeStruct((batch, seq, width), bf)
    cols_of = lambda rows: jax.ShapeDtypeStruct((batch, seq // tm, rows, tm), bf)
    return pl.pallas_call(
        _fox_proj_kernel,
        grid=(batch, seq // tm),
        in_specs=[
            row_spec(d),
            pl.BlockSpec((None, 1, 3 * d), lambda b, i: (b, 0, 0)),
            full((d, d)), full((d, d)), full((d, d)),
            full((d, LANES)), full((1, LANES)), full((1, LANES)),
        ],
        out_specs=[col_spec(d), row_spec(d), col_spec(d), col_spec(LANES), row_spec(LANES)],
        out_shape=[cols_of(d), rows_of(d), cols_of(d), cols_of(LANES), rows_of(LANES)],
        scratch_shapes=[pltpu.VMEM((CARRY_ROWS, LANES), jnp.float32)],
        compiler_params=_cparams(("arbitrary", "arbitrary")),
        name="fox_proj",
    )(x, mod, wqt, wk, wvt, w_f, bias, jnp.asarray(code))


def _rope(x, cos, sin_signed):
    return x * cos + pltpu.roll(x, MLA_ROPE, axis=1) * sin_signed


def _rope_t(xt, cos_t, sin_signed_t):
    partner = jnp.concatenate([xt[MLA_ROPE:, :], xt[:MLA_ROPE, :]], axis=0)
    return xt * cos_t + partner * sin_signed_t


def _mla_proj_kernel(x_ref, mod_ref, wa_ref, gq_ref, gkv_ref, wuqt_ref, wkn_ref, wvt_ref,
                     cos_ref, sin_ref, cost_ref, sint_ref, qt_ref, kn_ref, kr_ref, vt_ref):
    h = _mod_norm(x_ref[...], mod_ref).astype(jnp.bfloat16)
    a = _dot(h, wa_ref[...])
    kv0 = MLA_Q_RANK
    kr0 = MLA_Q_RANK + MLA_KV_RANK
    c_q = (_rms(a[:, 0:kv0]) * gq_ref[...]).astype(jnp.bfloat16)
    c_kv = (_rms(a[:, kv0:kr0]) * gkv_ref[...]).astype(jnp.bfloat16)
    kr_ref[...] = _rope(a[:, kr0:kr0 + LANES], cos_ref[...], sin_ref[...]).astype(kr_ref.dtype)
    kn_ref[...] = _dot(c_kv, wkn_ref[...]).astype(kn_ref.dtype)
    vt_ref[...] = _dot_nt(wvt_ref[...], c_kv).astype(vt_ref.dtype)
    scale = (MLA_NOPE + MLA_ROPE) ** -0.5 * LOG2E
    cos_t = cost_ref[...]
    sin_t = sint_ref[...]
    for p in range(N_PAIRS):
        r0 = p * 2 * LANES
        qp = _dot_nt(wuqt_ref[r0:r0 + 2 * LANES, :], c_q)
        qt_ref[r0:r0 + LANES, :] = (qp[0:LANES, :] * scale).astype(qt_ref.dtype)
        qt_ref[r0 + LANES:r0 + 2 * LANES, :] = (
            _rope_t(qp[LANES:2 * LANES, :], cos_t, sin_t) * scale).astype(qt_ref.dtype)


def _mla_weight_layout(w_a, w_uq, w_ukv):
    half = MLA_ROPE // 2
    kr0 = MLA_Q_RANK + MLA_KV_RANK
    x1 = np.arange(half)
    x2 = half + np.arange(half)
    kr_cols = kr0 + np.concatenate([x1, x1, x2, x2] * 2)
    wa_cols = np.concatenate([np.arange(kr0), kr_cols])
    q_cols = []
    qd = MLA_NOPE + MLA_ROPE
    for p in range(N_PAIRS):
        h0, h1 = 2 * p * qd, (2 * p + 1) * qd
        rope = np.concatenate([h0 + MLA_NOPE + x1, h1 + MLA_NOPE + x1,
                               h0 + MLA_NOPE + x2, h1 + MLA_NOPE + x2])
        q_cols.append(np.concatenate([h0 + np.arange(MLA_NOPE), h1 + np.arange(MLA_NOPE),
                                      rope, rope]))
    q_cols = np.concatenate(q_cols)
    kvd = MLA_NOPE + MLA_V
    kn_cols = np.concatenate([hh * kvd + np.arange(MLA_NOPE) for hh in range(N_HEADS)])
    v_cols = np.concatenate([hh * kvd + MLA_NOPE + np.arange(MLA_V) for hh in range(N_HEADS)])
    bf = jnp.bfloat16
    return (w_a[:, wa_cols].astype(bf), w_uq[:, q_cols].T.astype(bf),
            w_ukv[:, kn_cols].astype(bf), w_ukv[:, v_cols].T.astype(bf))


def _rope_tables(seq):
    half = MLA_ROPE // 2
    pos = jnp.arange(seq, dtype=jnp.float32)
    inv_freq = ROPE_BASE ** (-jnp.arange(0, MLA_ROPE, 2, dtype=jnp.float32) / MLA_ROPE)
    ang = pos[:, None] * inv_freq[None, :]
    cos = jnp.tile(jnp.cos(ang), (1, LANES // half))
    sign = np.where((np.arange(LANES) % (4 * half)) < 2 * half, -1.0, 1.0).astype(np.float32)
    sin = jnp.tile(jnp.sin(ang), (1, LANES // half)) * sign[None, :]
    return cos, sin


def _mla_projection(x, mod, w_a, g_q, g_kv, w_uq, w_ukv):
    batch, seq, _ = x.shape
    wa, wuqt, wkn, wvt = _mla_weight_layout(w_a, w_uq, w_ukv)
    cos, sin = _rope_tables(seq)
    tm = PROJ_TM
    d = D_MODEL
    row_spec = lambda width: pl.BlockSpec((None, tm, width), lambda b, i: (b, i, 0))
    col_spec = lambda rows: pl.BlockSpec((None, None, rows, tm), lambda b, i: (b, i, 0, 0))
    full = lambda shape: pl.BlockSpec(shape, lambda b, i: tuple(0 for _ in shape))
    tab = pl.BlockSpec((tm, LANES), lambda b, i: (i, 0))
    tab_t = pl.BlockSpec((LANES, tm), lambda b, i: (0, i))
    bf = jnp.bfloat16
    return pl.pallas_call(
        _mla_proj_kernel,
        grid=(batch, seq // tm),
        in_specs=[
            row_spec(d),
            pl.BlockSpec((None, 1, 3 * d), lambda b, i: (b, 0, 0)),
            full(wa.shape), full((1, MLA_Q_RANK)), full((1, MLA_KV_RANK)),
            full(wuqt.shape), full(wkn.shape), full(wvt.shape), tab, tab, tab_t, tab_t,
        ],
        out_specs=[col_spec(2 * d), row_spec(d), row_spec(LANES), col_spec(d)],
        out_shape=[jax.ShapeDtypeStruct((batch, seq // tm, 2 * d, tm), bf),
                   jax.ShapeDtypeStruct((batch, seq, d), bf),
                   jax.ShapeDtypeStruct((batch, seq, LANES), bf),
                   jax.ShapeDtypeStruct((batch, seq // tm, d, tm), bf)],
        compiler_params=_cparams(("arbitrary", "arbitrary")),
        name="mla_proj",
    )(x, mod, wa, g_q.reshape(1, -1), g_kv.reshape(1, -1), wuqt, wkn, wvt, cos, sin, cos.T, sin.T)


def _head_row_masks(pair, fox):
    row = lax.broadcasted_iota(jnp.int32, (2 * LANES, 1), 0)
    masks = []
    for hh in range(2):
        m = (row >= hh * HEAD_DIM) & (row < (hh + 1) * HEAD_DIM)
        r = row - LANES
        if fox:
            lo = (2 * pair + hh) * EXT_LANES
            m = m | ((r >= lo) & (r < lo + EXT_LANES))
        else:
            in_rope = (r >= 0) & (r < 2 * MLA_ROPE)
            m = m | (in_rope & (((r >> 4) & 1) == hh))
        masks.append(m)
    return masks


def _attn_kernel(*refs, fox):
    if fox:
        qt_ref, qxt_ref, k_ref, kx_ref, vt_ref, o_ref, m_ref, acc_ref, s_ref, cmax_ref = refs
    else:
        qt_ref, k_ref, kx_ref, vt_ref, o_ref, m_ref, acc_ref, s_ref, cmax_ref = refs
    pair = pl.program_id(1)
    qi = pl.program_id(2)
    tk = ATTN_TK
    tq = ATTN_TQ
    last = 2 * qi + 1

    qt = jnp.concatenate([qt_ref[0], qt_ref[1]], axis=1)
    if fox:
        qt = jnp.concatenate([qt, jnp.concatenate([qxt_ref[0], qxt_ref[1]], axis=1)], axis=0)
    q_heads = [jnp.where(m, qt, jnp.zeros_like(qt)) for m in _head_row_masks(pair, fox)]

    m_ref[...] = jnp.full(m_ref.shape, -jnp.inf, jnp.float32)
    acc_ref[...] = jnp.zeros_like(acc_ref)
    vrow = lax.broadcasted_iota(jnp.int32, (LANES, 1), 0)

    n_piece = 2
    qw = tq // n_piece
    kw = tk // n_piece

    def load_keys(j):
        start = pl.multiple_of(j * tk, tk)
        return jnp.concatenate([k_ref[pl.ds(start, tk), :], kx_ref[pl.ds(start, tk), :]], axis=1)

    def load_values(j):
        vt = vt_ref[j]
        ones = jnp.ones_like(vt)
        return [jnp.where(vrow < HEAD_DIM, vt, ones), jnp.where(vrow >= HEAD_DIM, vt, ones)]

    def score_piece(k_t, slot, hh, c, key_offset=None, unmasked=False):
        cols = slice(c * qw, (c + 1) * qw)
        s = _dot(k_t, q_heads[hh][:, cols])
        if key_offset is not None:
            key = lax.broadcasted_iota(jnp.int32, (tk, qw), 0) + key_offset
            qry = lax.broadcasted_iota(jnp.int32, (tk, qw), 1) + c * qw
            s = jnp.where((qry >= key) | unmasked, s, -jnp.inf)
        s_ref[slot, hh, :, cols] = s
        cmax_ref[slot, hh, :, cols] = jnp.max(s, axis=0, keepdims=True)

    def new_max(slot, hh):
        m_old = m_ref[hh]
        m_new = jnp.maximum(m_old, cmax_ref[slot, hh])
        m_ref[hh] = m_new
        return m_new, jnp.exp2(m_old - m_new)

    def pv_piece(vts, slot, hh, c, m_new):
        rows = slice(c * kw, (c + 1) * kw)
        p = jnp.exp2((s_ref[slot, hh, rows, :] - m_new).astype(jnp.bfloat16))
        return _dot(vts[hh][:, rows], p)

    def advance(j, slot, **mask):
        k_t = load_keys(j + 1)
        vts = load_values(j)
        for hh in range(2):
            m_new, alpha = new_max(slot, hh)
            pv = None
            for c in range(n_piece):
                score_piece(k_t, 1 - slot, hh, c, **mask)
                d = pv_piece(vts, slot, hh, c, m_new)
                pv = d if pv is None else pv + d
            acc_ref[hh] = alpha * acc_ref[hh] + pv

    k_t = load_keys(0)
    for hh in range(2):
        for c in range(n_piece):
            score_piece(k_t, 0, hh, c, key_offset=0, unmasked=qi > 0)

    def body(i, carry):
        advance(2 * i, 0)
        advance(2 * i + 1, 1)
        return carry

    lax.fori_loop(0, qi - 1, body, 0)

    @pl.when(qi > 0)
    def _():
        advance(last - 3, 0)
        advance(last - 2, 1, key_offset=0)

    advance(last - 1, 0, key_offset=tk)
    vts = load_values(last)
    for hh in range(2):
        m_new, alpha = new_max(1, hh)
        pv = pv_piece(vts, 1, hh, 0, m_new)
        for c in range(1, n_piece):
            pv = pv + pv_piece(vts, 1, hh, c, m_new)
        acc_ref[hh] = alpha * acc_ref[hh] + pv

    a0 = acc_ref[0]
    a1 = acc_ref[1]
    first = vrow < HEAD_DIM
    num = jnp.where(first, a0, a1)
    den = jnp.where(first, a0[HEAD_DIM:HEAD_DIM + 1, :], a1[0:1, :])
    o_ref[...] = (num / den).T.astype(o_ref.dtype)


def _attention(qt, k, kx, vt, *, qxt=None):
    batch, seq, _ = k.shape
    tq, tk = ATTN_TQ, ATTN_TK
    assert tq == 2 * tk and vt.shape[1:] == (seq // tk, D_MODEL, tk) and qt.shape[3] == tk
    fox = qxt is not None
    rows = qt.shape[2] // N_PAIRS
    shared_rows = lambda rows: pl.BlockSpec((None, rows, LANES), lambda b, p, i: (b, 0, 0))
    in_specs = [pl.BlockSpec((None, 2, rows, tk), lambda b, p, i: (b, i, p, 0))]
    args = [qt]
    if fox:
        in_specs.append(pl.BlockSpec((None, 2, LANES, tk), lambda b, p, i: (b, i, 0, 0)))
        args.append(qxt)
    in_specs += [pl.BlockSpec((None, seq, LANES), lambda b, p, i: (b, 0, p)),
                 shared_rows(seq),
                 pl.BlockSpec((None, seq // tk, LANES, tk), lambda b, p, i: (b, 0, p, 0))]
    args += [k, kx, vt]
    return pl.pallas_call(
        functools.partial(_attn_kernel, fox=fox),
        grid=(batch, N_PAIRS, seq // tq),
        in_specs=in_specs,
        out_specs=pl.BlockSpec((None, tq, LANES), lambda b, p, i: (b, i, p)),
        out_shape=jax.ShapeDtypeStruct((batch, seq, D_MODEL), jnp.bfloat16),
        scratch_shapes=[pltpu.VMEM((2, 1, tq), jnp.float32),
                        pltpu.VMEM((2, LANES, tq), jnp.float32),
                        pltpu.VMEM((2, 2, tk, tq), jnp.float32),
                        pltpu.VMEM((2, 2, 1, tq), jnp.float32)],
        compiler_params=_cparams(("arbitrary", "arbitrary", "arbitrary")),
        name="attn_fox" if fox else "attn_mla",
    )(*args)


def _ffn_kernel(x_ref, o_ref, moda_ref, modf_ref, wo_ref, wg_ref, wv_ref, cwg_ref, cwv_ref,
                cbg_ref, cbv_ref, wout_ref, fg_ref, out_ref,
                xnew_ref, h_ref, acc_ref, ug_ref, uv_ref, carry_ref, *, final_norm):
    i = pl.program_id(1)
    f = pl.program_id(2)
    nf = pl.num_programs(2)
    tm = x_ref.shape[0]

    @pl.when(f == 0)
    def _():
        gate_a = moda_ref[:, 2 * D_MODEL:3 * D_MODEL]
        xn = x_ref[...] + gate_a * _dot(o_ref[...], wo_ref[...])
        xnew_ref[...] = xn
        h_ref[...] = _mod_norm(xn, modf_ref).astype(h_ref.dtype)
        acc_ref[...] = jnp.zeros_like(acc_ref)

    h = h_ref[...]

    def conv(u_ref, w_ref, cw_ref, cb_ref, slot):
        prev = carry_ref[f, slot]
        u_ref[0:CARRY_ROWS, :] = jnp.where(i == 0, jnp.zeros_like(prev), prev)
        u_ref[CARRY_ROWS:CARRY_ROWS + tm, :] = _dot(h, w_ref[...])
        carry_ref[f, slot] = u_ref[tm:tm + CARRY_ROWS, :]
        y = cb_ref[...]
        for tap in range(CONV_WIDTH):
            off = CARRY_ROWS - (CONV_WIDTH - 1) + tap
            y = y + cw_ref[tap:tap + 1, :] * u_ref[off:off + tm, :]
        return y

    yg = conv(ug_ref, wg_ref, cwg_ref, cbg_ref, 0)
    yv = conv(uv_ref, wv_ref, cwv_ref, cbv_ref, 1)
    act = (yg * (1.0 / (1.0 + jnp.exp(-yg)))) * yv
    acc_ref[...] += _dot(act.astype(jnp.bfloat16), wout_ref[...])

    @pl.when(f == nf - 1)
    def _():
        gate_f = modf_ref[:, 2 * D_MODEL:3 * D_MODEL]
        y = xnew_ref[...] + gate_f * acc_ref[...]
        if final_norm:
            y = _rms(y) * fg_ref[...]
        out_ref[...] = y


def _mixer_out_and_ffn(x, o, mod_a, mod_f, w_o, w_in, conv_w, conv_b, w_out, final_g, final_norm):
    batch, seq, _ = x.shape
    tm, fc = FFN_TM, FFN_FC
    nf = D_FF // fc
    bf = jnp.bfloat16
    w_in_b = w_in.astype(bf)
    row = lambda: pl.BlockSpec((None, tm, D_MODEL), lambda b, i, f: (b, i, 0))
    mod_spec = lambda: pl.BlockSpec((None, 1, 3 * D_MODEL), lambda b, i, f: (b, 0, 0))
    gate_cols = lambda r: pl.BlockSpec((r, fc), lambda b, i, f: (0, f))
    val_cols = lambda r: pl.BlockSpec((r, fc), lambda b, i, f: (0, nf + f))
    conv_b2 = conv_b.reshape(1, -1)
    return pl.pallas_call(
        functools.partial(_ffn_kernel, final_norm=final_norm),
        grid=(batch, seq // tm, nf),
        in_specs=[
            row(), row(), mod_spec(), mod_spec(),
            pl.BlockSpec((D_MODEL, D_MODEL), lambda b, i, f: (0, 0)),
            gate_cols(D_MODEL), val_cols(D_MODEL),
            gate_cols(CONV_WIDTH), val_cols(CONV_WIDTH),
            gate_cols(1), val_cols(1),
            pl.BlockSpec((fc, D_MODEL), lambda b, i, f: (f, 0)),
            pl.BlockSpec((1, D_MODEL), lambda b, i, f: (0, 0)),
        ],
        out_specs=row(),
        out_shape=jax.ShapeDtypeStruct((batch, seq, D_MODEL), jnp.float32),
        scratch_shapes=[
            pltpu.VMEM((tm, D_MODEL), jnp.float32),
            pltpu.VMEM((tm, D_MODEL), bf),
            pltpu.VMEM((tm, D_MODEL), jnp.float32),
            pltpu.VMEM((tm + CARRY_ROWS, fc), jnp.float32),
            pltpu.VMEM((tm + CARRY_ROWS, fc), jnp.float32),
            pltpu.VMEM((nf, 2, CARRY_ROWS, fc), jnp.float32),
        ],
        compiler_params=_cparams(("arbitrary", "arbitrary", "arbitrary")),
        name="mixer_out_ffn",
    )(x, o, mod_a, mod_f, w_o.astype(bf), w_in_b, w_in_b, conv_w, conv_w, conv_b2, conv_b2,
      w_out.astype(bf), final_g.reshape(1, -1))


def kernel(x, c, ada_w, ada_b, fox_w_in, fox_b_f, fox_w_o, mla_w_a, mla_g_q, mla_g_kv, mla_w_uq,
           mla_w_ukv, mla_w_o, ffn_w_in, ffn_conv_w, ffn_conv_b, ffn_w_out, final_g):
    depth = ada_w.shape[0]
    mods = _ada_modulation(c, ada_w, ada_b)
    for layer in range(depth):
        j = layer // 2
        mod_a, mod_f = mods[2 * layer], mods[2 * layer + 1]
        if layer % 2 == 0:
            qt, k, vt, qxt, kx = _fox_projection(x, mod_a, fox_w_in[j], fox_b_f[j])
            o = _attention(qt, k, kx, vt, qxt=qxt)
            w_o = fox_w_o[j]
        else:
            qt, kn, kr, vt = _mla_projection(x, mod_a, mla_w_a[j], mla_g_q[j], mla_g_kv[j],
                                             mla_w_uq[j], mla_w_ukv[j])
            o = _attention(qt, kn, kr, vt)
            w_o = mla_w_o[j]
        x = _mixer_out_and_ffn(x, o, mod_a, mod_f, w_o, ffn_w_in[layer], ffn_conv_w[layer],
                               ffn_conv_b[layer], ffn_w_out[layer], final_g,
                               final_norm=(layer == depth - 1))
    return x
```

```python
import functools

import jax
import jax.numpy as jnp
import numpy as np
from jax import lax
from jax.experimental import pallas as pl
from jax.experimental.pallas import tpu as pltpu

D_MODEL = 1024
N_HEADS = 16
HEAD_DIM = 64
N_PAIRS = N_HEADS // 2
LANES = 128
NORM_EPS = 1e-6
MLA_NOPE = 64
MLA_ROPE = 32
MLA_V = 64
MLA_Q_RANK = 384
MLA_KV_RANK = 256
ROPE_BASE = 10000.0
D_FF = 2816
CONV_WIDTH = 3

VMEM_LIMIT_BYTES = 56 * 1024 * 1024

ADA_TN = 1024
PROJ_TM = 512
ATTN_TK = 512
ATTN_TQ = 1024
FFN_TM = 512
FFN_FC = 256
FFN_ROW_PIECES = 4
CARRY_ROWS = 8
EXT_LANES = 6
LOG2E = 1.4426950408889634


def _cparams(semantics, flags=None):
    return pltpu.CompilerParams(dimension_semantics=semantics,
                                vmem_limit_bytes=VMEM_LIMIT_BYTES, flags=flags)


def _rms(x):
    return x * lax.rsqrt(jnp.mean(x * x, axis=-1, keepdims=True) + NORM_EPS)


def _mod_norm(x, mod_ref):
    shift = mod_ref[:, 0:D_MODEL]
    scale = mod_ref[:, D_MODEL:2 * D_MODEL]
    return _rms(x) * (1.0 + scale) + shift


def _dot(a, b):
    return jnp.dot(a, b, preferred_element_type=jnp.float32)


def _dot_nt(a, b):
    return lax.dot_general(a, b, (((1,), (1,)), ((), ())), preferred_element_type=jnp.float32)


def _ada_kernel(c_ref, w_ref, b_ref, o_ref):
    c = c_ref[...]
    sc = (c * (1.0 / (1.0 + jnp.exp(-c)))).astype(jnp.bfloat16)
    o_ref[...] = _dot(sc, w_ref[...].astype(jnp.bfloat16)) + b_ref[...]


def _ada_modulation(c, ada_w, ada_b):
    n_mod = ada_w.shape[0] * ada_w.shape[1]
    batch = c.shape[0]
    rows = 8
    c_pad = jnp.zeros((rows, D_MODEL), jnp.float32).at[:batch].set(c)
    w = ada_w.reshape(n_mod, D_MODEL, 3 * D_MODEL)
    b = ada_b.reshape(n_mod, 1, 3 * D_MODEL)
    out = pl.pallas_call(
        _ada_kernel,
        grid=(n_mod, 3 * D_MODEL // ADA_TN),
        in_specs=[
            pl.BlockSpec((rows, D_MODEL), lambda l, n: (0, 0)),
            pl.BlockSpec((None, D_MODEL, ADA_TN), lambda l, n: (l, 0, n)),
            pl.BlockSpec((None, 1, ADA_TN), lambda l, n: (l, 0, n)),
        ],
        out_specs=pl.BlockSpec((None, rows, ADA_TN), lambda l, n: (l, 0, n)),
        out_shape=jax.ShapeDtypeStruct((n_mod, rows, 3 * D_MODEL), jnp.float32),
        compiler_params=_cparams(("arbitrary", "arbitrary")),
        name="ada_mod",
    )(c_pad, w, b)
    return out[:, :batch].reshape(n_mod, batch, 1, 3 * D_MODEL)


def _split3(x):
    hi = x.astype(jnp.bfloat16).astype(jnp.float32)
    r = x - hi
    mid = r.astype(jnp.bfloat16).astype(jnp.float32)
    lo = (r - mid).astype(jnp.bfloat16).astype(jnp.float32)
    return hi, mid, lo


def _fox_proj_kernel(x_ref, mod_ref, wqt_ref, wk_ref, wvt_ref, wf_ref, bf_ref, code_ref,
                     qt_ref, k_ref, vt_ref, qxt_ref, kx_ref, carry_ref):
    i = pl.program_id(1)
    tm = x_ref.shape[0]
    h = _mod_norm(x_ref[...], mod_ref).astype(jnp.bfloat16)
    qt_ref[...] = (_dot_nt(wqt_ref[...], h) * (HEAD_DIM ** -0.5 * LOG2E)).astype(qt_ref.dtype)
    k_ref[...] = _dot(h, wk_ref[...]).astype(k_ref.dtype)
    vt_ref[...] = _dot_nt(wvt_ref[...], h).astype(vt_ref.dtype)

    z = _dot(h, wf_ref[...]) + bf_ref[...]
    log_f = jnp.minimum(z, 0.0) - jnp.log(1.0 + jnp.exp(-jnp.abs(z)))

    @pl.when(i == 0)
    def _():
        carry_ref[...] = jnp.zeros_like(carry_ref)

    row = lax.broadcasted_iota(jnp.int32, (tm, tm), 0)
    col = lax.broadcasted_iota(jnp.int32, (tm, tm), 1)
    tri = (row >= col).astype(jnp.bfloat16)
    bf = jnp.bfloat16
    hi, mid, lo = _split3(log_f)
    cum = (_dot(tri, hi.astype(bf)) + _dot(tri, mid.astype(bf))) + _dot(tri, lo.astype(bf))
    cum = cum + carry_ref[0:1, :]
    carry_ref[...] = jnp.broadcast_to(cum[tm - 1:tm, :], carry_ref.shape)

    hi, mid, lo = _split3(cum * LOG2E)
    code = code_ref[...]
    one = jnp.ones_like(hi)
    zero = jnp.zeros_like(hi)
    qx = jnp.where(code == 0, hi, jnp.where(code == 1, mid, jnp.where(code == 2, lo,
                   jnp.where(code < EXT_LANES, one, zero))))
    kx = jnp.where(code < 3, one, jnp.where(code == 3, -hi, jnp.where(code == 4, -mid,
                   jnp.where(code == 5, -lo, zero))))
    qxt_ref[...] = qx.T.astype(qxt_ref.dtype)
    kx_ref[...] = kx.astype(kx_ref.dtype)


def _fox_projection(x, mod, w_in, b_f):
    batch, seq, _ = x.shape
    bf = jnp.bfloat16
    d = D_MODEL
    wqt = w_in[:, 0:d].T.astype(bf)
    wk = w_in[:, d:2 * d].astype(bf)
    wvt = w_in[:, 2 * d:3 * d].T.astype(bf)
    lane = np.arange(LANES)
    used = lane < EXT_LANES * N_HEADS
    head_of_lane = np.where(used, lane // EXT_LANES, 0)
    code = np.where(used, lane % EXT_LANES, EXT_LANES).astype(np.int32)[None, :]
    w_f = jnp.where(used[None, :], w_in[:, 3 * d:][:, head_of_lane], 0.0).astype(bf)
    bias = jnp.where(used, b_f[head_of_lane], 0.0)[None, :]
    tm = PROJ_TM
    row_spec = lambda width: pl.BlockSpec((None, tm, width), lambda b, i: (b, i, 0))
    col_spec = lambda rows: pl.BlockSpec((None, None, rows, tm), lambda b, i: (b, i, 0, 0))
    full = lambda shape: pl.BlockSpec(shape, lambda b, i: tuple(0 for _ in shape))
    rows_of = lambda width: jax.ShapeDtypeStruct((batch, seq, width), bf)
    cols_of = lambda rows: jax.ShapeDtypeStruct((batch, seq // tm, rows, tm), bf)
    return pl.pallas_call(
        _fox_proj_kernel,
        grid=(batch, seq // tm),
        in_specs=[
            row_spec(d),
            pl.BlockSpec((None, 1, 3 * d), lambda b, i: (b, 0, 0)),
            full((d, d)), full((d, d)), full((d, d)),
            full((d, LANES)), full((1, LANES)), full((1, LANES)),
        ],
        out_specs=[col_spec(d), row_spec(d), col_spec(d), col_spec(LANES), row_spec(LANES)],
        out_shape=[cols_of(d), rows_of(d), cols_of(d), cols_of(LANES), rows_of(LANES)],
        scratch_shapes=[pltpu.VMEM((CARRY_ROWS, LANES), jnp.float32)],
        compiler_params=_cparams(("arbitrary", "arbitrary")),
        name="fox_proj",
    )(x, mod, wqt, wk, wvt, w_f, bias, jnp.asarray(code))


def _rope(x, cos, sin_signed):
    return x * cos + pltpu.roll(x, MLA_ROPE, axis=1) * sin_signed


def _rope_t(xt, cos_t, sin_signed_t):
    partner = jnp.concatenate([xt[MLA_ROPE:, :], xt[:MLA_ROPE, :]], axis=0)
    return xt * cos_t + partner * sin_signed_t


def _mla_proj_kernel(x_ref, mod_ref, wa_ref, gq_ref, gkv_ref, wuqt_ref, wkn_ref, wvt_ref,
                     cos_ref, sin_ref, cost_ref, sint_ref, qt_ref, kn_ref, kr_ref, vt_ref):
    h = _mod_norm(x_ref[...], mod_ref).astype(jnp.bfloat16)
    a = _dot(h, wa_ref[...])
    kv0 = MLA_Q_RANK
    kr0 = MLA_Q_RANK + MLA_KV_RANK
    c_q = (_rms(a[:, 0:kv0]) * gq_ref[...]).astype(jnp.bfloat16)
    c_kv = (_rms(a[:, kv0:kr0]) * gkv_ref[...]).astype(jnp.bfloat16)
    kr_ref[...] = _rope(a[:, kr0:kr0 + LANES], cos_ref[...], sin_ref[...]).astype(kr_ref.dtype)
    kn_ref[...] = _dot(c_kv, wkn_ref[...]).astype(kn_ref.dtype)
    vt_ref[...] = _dot_nt(wvt_ref[...], c_kv).astype(vt_ref.dtype)
    scale = (MLA_NOPE + MLA_ROPE) ** -0.5 * LOG2E
    cos_t = cost_ref[...]
    sin_t = sint_ref[...]
    for p in range(N_PAIRS):
        r0 = p * 2 * LANES
        qp = _dot_nt(wuqt_ref[r0:r0 + 2 * LANES, :], c_q)
        qt_ref[r0:r0 + LANES, :] = (qp[0:LANES, :] * scale).astype(qt_ref.dtype)
        qt_ref[r0 + LANES:r0 + 2 * LANES, :] = (
            _rope_t(qp[LANES:2 * LANES, :], cos_t, sin_t) * scale).astype(qt_ref.dtype)


def _mla_weight_layout(w_a, w_uq, w_ukv):
    half = MLA_ROPE // 2
    kr0 = MLA_Q_RANK + MLA_KV_RANK
    x1 = np.arange(half)
    x2 = half + np.arange(half)
    kr_cols = kr0 + np.concatenate([x1, x1, x2, x2] * 2)
    wa_cols = np.concatenate([np.arange(kr0), kr_cols])
    q_cols = []
    qd = MLA_NOPE + MLA_ROPE
    for p in range(N_PAIRS):
        h0, h1 = 2 * p * qd, (2 * p + 1) * qd
        rope = np.concatenate([h0 + MLA_NOPE + x1, h1 + MLA_NOPE + x1,
                               h0 + MLA_NOPE + x2, h1 + MLA_NOPE + x2])
        q_cols.append(np.concatenate([h0 + np.arange(MLA_NOPE), h1 + np.arange(MLA_NOPE),
                                      rope, rope]))
    q_cols = np.concatenate(q_cols)
    kvd = MLA_NOPE + MLA_V
    kn_cols = np.concatenate([hh * kvd + np.arange(MLA_NOPE) for hh in range(N_HEADS)])
    v_cols = np.concatenate([hh * kvd + MLA_NOPE + np.arange(MLA_V) for hh in range(N_HEADS)])
    bf = jnp.bfloat16
    return (w_a[:, wa_cols].astype(bf), w_uq[:, q_cols].T.astype(bf),
            w_ukv[:, kn_cols].astype(bf), w_ukv[:, v_cols].T.astype(bf))


def _rope_tables(seq):
    half = MLA_ROPE // 2
    pos = jnp.arange(seq, dtype=jnp.float32)
    inv_freq = ROPE_BASE ** (-jnp.arange(0, MLA_ROPE, 2, dtype=jnp.float32) / MLA_ROPE)
    ang = pos[:, None] * inv_freq[None, :]
    cos = jnp.tile(jnp.cos(ang), (1, LANES // half))
    sign = np.where((np.arange(LANES) % (4 * half)) < 2 * half, -1.0, 1.0).astype(np.float32)
    sin = jnp.tile(jnp.sin(ang), (1, LANES // half)) * sign[None, :]
    return cos, sin


def _mla_projection(x, mod, w_a, g_q, g_kv, w_uq, w_ukv):
    batch, seq, _ = x.shape
    wa, wuqt, wkn, wvt = _mla_weight_layout(w_a, w_uq, w_ukv)
    cos, sin = _rope_tables(seq)
    tm = PROJ_TM
    d = D_MODEL
    row_spec = lambda width: pl.BlockSpec((None, tm, width), lambda b, i: (b, i, 0))
    col_spec = lambda rows: pl.BlockSpec((None, None, rows, tm), lambda b, i: (b, i, 0, 0))
    full = lambda shape: pl.BlockSpec(shape, lambda b, i: tuple(0 for _ in shape))
    tab = pl.BlockSpec((tm, LANES), lambda b, i: (i, 0))
    tab_t = pl.BlockSpec((LANES, tm), lambda b, i: (0, i))
    bf = jnp.bfloat16
    return pl.pallas_call(
        _mla_proj_kernel,
        grid=(batch, seq // tm),
        in_specs=[
            row_spec(d),
            pl.BlockSpec((None, 1, 3 * d), lambda b, i: (b, 0, 0)),
            full(wa.shape), full((1, MLA_Q_RANK)), full((1, MLA_KV_RANK)),
            full(wuqt.shape), full(wkn.shape), full(wvt.shape), tab, tab, tab_t, tab_t,
        ],
        out_specs=[col_spec(2 * d), row_spec(d), row_spec(LANES), col_spec(d)],
        out_shape=[jax.ShapeDtypeStruct((batch, seq // tm, 2 * d, tm), bf),
                   jax.ShapeDtypeStruct((batch, seq, d), bf),
                   jax.ShapeDtypeStruct((batch, seq, LANES), bf),
                   jax.ShapeDtypeStruct((batch, seq // tm, d, tm), bf)],
        compiler_params=_cparams(("arbitrary", "arbitrary")),
        name="mla_proj",
    )(x, mod, wa, g_q.reshape(1, -1), g_kv.reshape(1, -1), wuqt, wkn, wvt, cos, sin, cos.T, sin.T)


def _head_row_masks(pair, fox):
    row = lax.broadcasted_iota(jnp.int32, (2 * LANES, 1), 0)
    masks = []
    for hh in range(2):
        m = (row >= hh * HEAD_DIM) & (row < (hh + 1) * HEAD_DIM)
        r = row - LANES
        if fox:
            lo = (2 * pair + hh) * EXT_LANES
            m = m | ((r >= lo) & (r < lo + EXT_LANES))
        else:
            in_rope = (r >= 0) & (r < 2 * MLA_ROPE)
            m = m | (in_rope & (((r >> 4) & 1) == hh))
        masks.append(m)
    return masks


def _attn_kernel(*refs, fox):
    if fox:
        qt_ref, qxt_ref, k_ref, kx_ref, vt_ref, o_ref, m_ref, acc_ref, s_ref, cmax_ref = refs
    else:
        qt_ref, k_ref, kx_ref, vt_ref, o_ref, m_ref, acc_ref, s_ref, cmax_ref = refs
    pair = pl.program_id(1)
    qi = pl.program_id(2)
    tk = ATTN_TK
    tq = ATTN_TQ
    last = 2 * qi + 1

    qt = jnp.concatenate([qt_ref[0], qt_ref[1]], axis=1)
    if fox:
        qt = jnp.concatenate([qt, jnp.concatenate([qxt_ref[0], qxt_ref[1]], axis=1)], axis=0)
    q_heads = [jnp.where(m, qt, jnp.zeros_like(qt)) for m in _head_row_masks(pair, fox)]

    m_ref[...] = jnp.full(m_ref.shape, -jnp.inf, jnp.float32)
    acc_ref[...] = jnp.zeros_like(acc_ref)
    vrow = lax.broadcasted_iota(jnp.int32, (LANES, 1), 0)

    n_piece = 2
    qw = tq // n_piece
    kw = tk // n_piece

    def load_keys(j):
        start = pl.multiple_of(j * tk, tk)
        return jnp.concatenate([k_ref[pl.ds(start, tk), :], kx_ref[pl.ds(start, tk), :]], axis=1)

    def load_values(j):
        vt = vt_ref[j]
        ones = jnp.ones_like(vt)
        return [jnp.where(vrow < HEAD_DIM, vt, ones), jnp.where(vrow >= HEAD_DIM, vt, ones)]

    def score_piece(k_t, slot, hh, c, key_offset=None, unmasked=False):
        cols = slice(c * qw, (c + 1) * qw)
        s = _dot(k_t, q_heads[hh][:, cols])
        if key_offset is not None:
            key = lax.broadcasted_iota(jnp.int32, (tk, qw), 0) + key_offset
            qry = lax.broadcasted_iota(jnp.int32, (tk, qw), 1) + c * qw
            s = jnp.where((qry >= key) | unmasked, s, -jnp.inf)
        s_ref[slot, hh, :, cols] = s
        cmax_ref[slot, hh, :, cols] = jnp.max(s, axis=0, keepdims=True)

    def new_max(slot, hh):
        m_old = m_ref[hh]
        m_new = jnp.maximum(m_old, cmax_ref[slot, hh])
        m_ref[hh] = m_new
        return m_new, jnp.exp2(m_old - m_new)

    def pv_piece(vts, slot, hh, c, m_new):
        rows = slice(c * kw, (c + 1) * kw)
        p = jnp.exp2((s_ref[slot, hh, rows, :] - m_new).astype(jnp.bfloat16))
        return _dot(vts[hh][:, rows], p)

    def advance(j, slot, **mask):
        k_t = load_keys(j + 1)
        vts = load_values(j)
        for hh in range(2):
            m_new, alpha = new_max(slot, hh)
            pv = None
            for c in range(n_piece):
                score_piece(k_t, 1 - slot, hh, c, **mask)
                d = pv_piece(vts, slot, hh, c, m_new)
                pv = d if pv is None else pv + d
            acc_ref[hh] = alpha * acc_ref[hh] + pv

    k_t = load_keys(0)
    for hh in range(2):
        for c in range(n_piece):
            score_piece(k_t, 0, hh, c, key_offset=0, unmasked=qi > 0)

    def body(i, carry):
        advance(2 * i, 0)
        advance(2 * i + 1, 1)
        return carry

    lax.fori_loop(0, qi - 1, body, 0)

    @pl.when(qi > 0)
    def _():
        advance(last - 3, 0)
        advance(last - 2, 1, key_offset=0)

    advance(last - 1, 0, key_offset=tk)
    vts = load_values(last)
    for hh in range(2):
        m_new, alpha = new_max(1, hh)
        pv = pv_piece(vts, 1, hh, 0, m_new)
        for c in range(1, n_piece):
            pv = pv + pv_piece(vts, 1, hh, c, m_new)
        acc_ref[hh] = alpha * acc_ref[hh] + pv

    a0 = acc_ref[0]
    a1 = acc_ref[1]
    first = vrow < HEAD_DIM
    num = jnp.where(first, a0, a1)
    den = jnp.where(first, a0[HEAD_DIM:HEAD_DIM + 1, :], a1[0:1, :])
    o_ref[...] = (num / den).T.astype(o_ref.dtype)


def _attention(qt, k, kx, vt, *, qxt=None):
    batch, seq, _ = k.shape
    tq, tk = ATTN_TQ, ATTN_TK
    assert tq == 2 * tk and vt.shape[1:] == (seq // tk, D_MODEL, tk) and qt.shape[3] == tk
    fox = qxt is not None
    rows = qt.shape[2] // N_PAIRS
    shared_rows = lambda rows: pl.BlockSpec((None, rows, LANES), lambda b, p, i: (b, 0, 0))
    in_specs = [pl.BlockSpec((None, 2, rows, tk), lambda b, p, i: (b, i, p, 0))]
    args = [qt]
    if fox:
        in_specs.append(pl.BlockSpec((None, 2, LANES, tk), lambda b, p, i: (b, i, 0, 0)))
        args.append(qxt)
    in_specs += [pl.BlockSpec((None, seq, LANES), lambda b, p, i: (b, 0, p)),
                 shared_rows(seq),
                 pl.BlockSpec((None, seq // tk, LANES, tk), lambda b, p, i: (b, 0, p, 0))]
    args += [k, kx, vt]
    return pl.pallas_call(
        functools.partial(_attn_kernel, fox=fox),
        grid=(batch, N_PAIRS, seq // tq),
        in_specs=in_specs,
        out_specs=pl.BlockSpec((None, tq, LANES), lambda b, p, i: (b, i, p)),
        out_shape=jax.ShapeDtypeStruct((batch, seq, D_MODEL), jnp.bfloat16),
        scratch_shapes=[pltpu.VMEM((2, 1, tq), jnp.float32),
                        pltpu.VMEM((2, LANES, tq), jnp.float32),
                        pltpu.VMEM((2, 2, tk, tq), jnp.float32),
                        pltpu.VMEM((2, 2, 1, tq), jnp.float32)],
        compiler_params=_cparams(("arbitrary", "arbitrary", "arbitrary")),
        name="attn_fox" if fox else "attn_mla",
    )(*args)


def _ffn_kernel(x_ref, o_ref, moda_ref, modf_ref, wo_ref, win_ref, cw_ref, cb_ref, wout_ref, fg_ref,
                out_ref, xn_ref, h_ref, acc_ref, u_ref, carry_ref, *, final_norm):
    i = pl.program_id(1)
    tm = x_ref.shape[0]
    fc = win_ref.shape[2]
    nf = wout_ref.shape[0]

    gate_a = moda_ref[:, 2 * D_MODEL:3 * D_MODEL]
    xn_ref[...] = x_ref[...] + gate_a * _dot(o_ref[...], wo_ref[...])
    h_ref[...] = _mod_norm(xn_ref[...], modf_ref).astype(h_ref.dtype)

    n_piece = FFN_ROW_PIECES
    rp = tm // n_piece

    def up(f, r):
        rows = slice(r * rp, (r + 1) * rp)
        for part in range(2):
            dst = u_ref.at[f % 2, part]
            if r == 0:
                prev = carry_ref[f, part]
                dst[0:CARRY_ROWS, :] = jnp.where(i == 0, jnp.zeros_like(prev), prev)
            dst[CARRY_ROWS + r * rp:CARRY_ROWS + (r + 1) * rp, :] = _dot(
                h_ref[rows, :], win_ref[part * nf + f])
            if r == n_piece - 1:
                carry_ref[f, part] = dst[tm:tm + CARRY_ROWS, :]

    def down(f, r):
        ys = []
        for part in range(2):
            src = u_ref.at[f % 2, part]
            c0 = part * D_FF + f * fc
            y = cb_ref[:, c0:c0 + fc]
            for tap in range(CONV_WIDTH):
                off = CARRY_ROWS - (CONV_WIDTH - 1) + tap + r * rp
                y = y + cw_ref[tap:tap + 1, c0:c0 + fc] * src[off:off + rp, :]
            ys.append(y)
        yg, yv = ys
        act = ((yg * (1.0 / (1.0 + jnp.exp(-yg)))) * yv).astype(jnp.bfloat16)
        rows = slice(r * rp, (r + 1) * rp)
        d = _dot(act, wout_ref[f])
        if f == 0:
            acc_ref[rows, :] = d
        else:
            acc_ref[rows, :] += d

    for r in range(n_piece):
        up(0, r)
    for f in range(nf):
        for r in range(n_piece):
            if f + 1 < nf:
                up(f + 1, r)
            down(f, r)

    gate_f = modf_ref[:, 2 * D_MODEL:3 * D_MODEL]
    y = xn_ref[...] + gate_f * acc_ref[...]
    if final_norm:
        y = _rms(y) * fg_ref[...]
    out_ref[...] = y


def _mixer_out_and_ffn(x, o, mod_a, mod_f, w_o, w_in, conv_w, conv_b, w_out, final_g, final_norm):
    batch, seq, _ = x.shape
    tm, fc = FFN_TM, FFN_FC
    nf = D_FF // fc
    bf = jnp.bfloat16
    d = D_MODEL
    w_in_c = w_in.astype(bf).reshape(d, 2 * nf, fc).transpose(1, 0, 2)
    w_out_c = w_out.astype(bf).reshape(nf, fc, d)
    row = lambda: pl.BlockSpec((None, tm, d), lambda b, i: (b, i, 0))
    mod_spec = lambda: pl.BlockSpec((None, 1, 3 * d), lambda b, i: (b, 0, 0))
    whole = lambda shape: pl.BlockSpec(shape, lambda b, i: tuple(0 for _ in shape),
                                       pipeline_mode=pl.Buffered(1))
    return pl.pallas_call(
        functools.partial(_ffn_kernel, final_norm=final_norm),
        grid=(batch, seq // tm),
        in_specs=[
            row(), row(), mod_spec(), mod_spec(),
            whole((d, d)), whole((2 * nf, d, fc)), whole((CONV_WIDTH, 2 * D_FF)),
            whole((1, 2 * D_FF)), whole((nf, fc, d)), whole((1, d)),
        ],
        out_specs=row(),
        out_shape=jax.ShapeDtypeStruct((batch, seq, d), jnp.float32),
        scratch_shapes=[
            pltpu.VMEM((tm, d), jnp.float32),
            pltpu.VMEM((tm, d), bf),
            pltpu.VMEM((tm, d), jnp.float32),
            pltpu.VMEM((2, 2, tm + CARRY_ROWS, fc), jnp.float32),
            pltpu.VMEM((nf, 2, CARRY_ROWS, fc), jnp.float32),
        ],
        compiler_params=_cparams(("arbitrary", "arbitrary")),
        name="mixer_out_ffn",
    )(x, o, mod_a, mod_f, w_o.astype(bf), w_in_c, conv_w, conv_b.reshape(1, -1), w_out_c,
      final_g.reshape(1, -1))


def kernel(x, c, ada_w, ada_b, fox_w_in, fox_b_f, fox_w_o, mla_w_a, mla_g_q, mla_g_kv, mla_w_uq,
           mla_w_ukv, mla_w_o, ffn_w_in, ffn_conv_w, ffn_conv_b, ffn_w_out, final_g):
    depth = ada_w.shape[0]
    mods = _ada_modulation(c, ada_w, ada_b)
    for layer in range(depth):
        j = layer // 2
        mod_a, mod_f = mods[2 * layer], mods[2 * layer + 1]
        if layer % 2 == 0:
            qt, k, vt, qxt, kx = _fox_projection(x, mod_a, fox_w_in[j], fox_b_f[j])
            o = _attention(qt, k, kx, vt, qxt=qxt)
            w_o = fox_w_o[j]
        else:
            qt, kn, kr, vt = _mla_projection(x, mod_a, mla_w_a[j], mla_g_q[j], mla_g_kv[j],
                                             mla_w_uq[j], mla_w_ukv[j])
            o = _attention(qt, kn, kr, vt)
            w_o = mla_w_o[j]
        x = _mixer_out_and_ffn(x, o, mod_a, mod_f, w_o, ffn_w_in[layer], ffn_conv_w[layer],
                               ffn_conv_b[layer], ffn_w_out[layer], final_g,
                               final_norm=(layer == depth - 1))
    return x
```

```python
import functools

import jax
import jax.numpy as jnp
import numpy as np
from jax import lax
from jax.experimental import pallas as pl
from jax.experimental.pallas import tpu as pltpu

D_MODEL = 1024
N_HEADS = 16
HEAD_DIM = 64
N_PAIRS = N_HEADS // 2
LANES = 128
NORM_EPS = 1e-6
MLA_NOPE = 64
MLA_ROPE = 32
MLA_V = 64
MLA_Q_RANK = 384
MLA_KV_RANK = 256
ROPE_BASE = 10000.0
D_FF = 2816
CONV_WIDTH = 3

VMEM_LIMIT_BYTES = 56 * 1024 * 1024

ADA_TN = 1024
PROJ_TM = 512
ATTN_TK = 512
ATTN_TQ = 1024
FFN_TM = 512
FFN_FC = 256
FFN_ROW_PIECES = 4
CARRY_ROWS = 8
EXT_LANES = 6
LOG2E = 1.4426950408889634


def _cparams(semantics, flags=None):
    return pltpu.CompilerParams(dimension_semantics=semantics,
                                vmem_limit_bytes=VMEM_LIMIT_BYTES, flags=flags)


def _rms(x):
    return x * lax.rsqrt(jnp.mean(x * x, axis=-1, keepdims=True) + NORM_EPS)


def _mod_norm(x, mod_ref):
    shift = mod_ref[:, 0:D_MODEL]
    scale = mod_ref[:, D_MODEL:2 * D_MODEL]
    return _rms(x) * (1.0 + scale) + shift


def _dot(a, b):
    return jnp.dot(a, b, preferred_element_type=jnp.float32)


def _dot_nt(a, b):
    return lax.dot_general(a, b, (((1,), (1,)), ((), ())), preferred_element_type=jnp.float32)


def _ada_kernel(c_ref, w_ref, b_ref, o_ref):
    c = c_ref[...]
    sc = (c * (1.0 / (1.0 + jnp.exp(-c)))).astype(jnp.bfloat16)
    o_ref[...] = _dot(sc, w_ref[...].astype(jnp.bfloat16)) + b_ref[...]


def _ada_modulation(c, ada_w, ada_b):
    n_mod = ada_w.shape[0] * ada_w.shape[1]
    batch = c.shape[0]
    rows = 8
    c_pad = jnp.zeros((rows, D_MODEL), jnp.float32).at[:batch].set(c)
    w = ada_w.reshape(n_mod, D_MODEL, 3 * D_MODEL)
    b = ada_b.reshape(n_mod, 1, 3 * D_MODEL)
    out = pl.pallas_call(
        _ada_kernel,
        grid=(n_mod, 3 * D_MODEL // ADA_TN),
        in_specs=[
            pl.BlockSpec((rows, D_MODEL), lambda l, n: (0, 0)),
            pl.BlockSpec((None, D_MODEL, ADA_TN), lambda l, n: (l, 0, n)),
            pl.BlockSpec((None, 1, ADA_TN), lambda l, n: (l, 0, n)),
        ],
        out_specs=pl.BlockSpec((None, rows, ADA_TN), lambda l, n: (l, 0, n)),
        out_shape=jax.ShapeDtypeStruct((n_mod, rows, 3 * D_MODEL), jnp.float32),
        compiler_params=_cparams(("arbitrary", "arbitrary")),
        name="ada_mod",
    )(c_pad, w, b)
    return out[:, :batch].reshape(n_mod, batch, 1, 3 * D_MODEL)


def _split3(x):
    hi = x.astype(jnp.bfloat16).astype(jnp.float32)
    r = x - hi
    mid = r.astype(jnp.bfloat16).astype(jnp.float32)
    lo = (r - mid).astype(jnp.bfloat16).astype(jnp.float32)
    return hi, mid, lo


def _fox_proj_kernel(x_ref, mod_ref, wqt_ref, wk_ref, wvt_ref, wf_ref, bf_ref, code_ref,
                     qt_ref, k_ref, vt_ref, qxt_ref, kx_ref, carry_ref):
    i = pl.program_id(1)
    tm = x_ref.shape[0]
    h = _mod_norm(x_ref[...], mod_ref).astype(jnp.bfloat16)
    qt_ref[...] = (_dot_nt(wqt_ref[...], h) * (HEAD_DIM ** -0.5 * LOG2E)).astype(qt_ref.dtype)
    k_ref[...] = _dot(h, wk_ref[...]).astype(k_ref.dtype)
    vt_ref[...] = _dot_nt(wvt_ref[...], h).astype(vt_ref.dtype)

    z = _dot(h, wf_ref[...]) + bf_ref[...]
    log_f = jnp.minimum(z, 0.0) - jnp.log(1.0 + jnp.exp(-jnp.abs(z)))

    @pl.when(i == 0)
    def _():
        carry_ref[...] = jnp.zeros_like(carry_ref)

    row = lax.broadcasted_iota(jnp.int32, (tm, tm), 0)
    col = lax.broadcasted_iota(jnp.int32, (tm, tm), 1)
    tri = (row >= col).astype(jnp.bfloat16)
    bf = jnp.bfloat16
    hi, mid, lo = _split3(log_f)
    cum = (_dot(tri, hi.astype(bf)) + _dot(tri, mid.astype(bf))) + _dot(tri, lo.astype(bf))
    cum = cum + carry_ref[0:1, :]
    carry_ref[...] = jnp.broadcast_to(cum[tm - 1:tm, :], carry_ref.shape)

    hi, mid, lo = _split3(cum * LOG2E)
    code = code_ref[...]
    one = jnp.ones_like(hi)
    zero = jnp.zeros_like(hi)
    qx = jnp.where(code == 0, hi, jnp.where(code == 1, mid, jnp.where(code == 2, lo,
                   jnp.where(code < EXT_LANES, one, zero))))
    kx = jnp.where(code < 3, one, jnp.where(code == 3, -hi, jnp.where(code == 4, -mid,
                   jnp.where(code == 5, -lo, zero))))
    qxt_ref[...] = qx.T.astype(qxt_ref.dtype)
    kx_ref[...] = kx.astype(kx_ref.dtype)


def _fox_projection(x, mod, w_in, b_f):
    batch, seq, _ = x.shape
    bf = jnp.bfloat16
    d = D_MODEL
    wqt = w_in[:, 0:d].T.astype(bf)
    wk = w_in[:, d:2 * d].astype(bf)
    wvt = w_in[:, 2 * d:3 * d].T.astype(bf)
    lane = np.arange(LANES)
    used = lane < EXT_LANES * N_HEADS
    head_of_lane = np.where(used, lane // EXT_LANES, 0)
    code = np.where(used, lane % EXT_LANES, EXT_LANES).astype(np.int32)[None, :]
    w_f = jnp.where(used[None, :], w_in[:, 3 * d:][:, head_of_lane], 0.0).astype(bf)
    bias = jnp.where(used, b_f[head_of_lane], 0.0)[None, :]
    tm = PROJ_TM
    row_spec = lambda width: pl.BlockSpec((None, tm, width), lambda b, i: (b, i, 0))
    col_spec = lambda rows: pl.BlockSpec((None, None, rows, tm), lambda b, i: (b, i, 0, 0))
    full = lambda shape: pl.BlockSpec(shape, lambda b, i: tuple(0 for _ in shape))
    rows_of = lambda width: jax.ShapeDtypeStruct((batch, seq, width), bf)
    cols_of = lambda rows: jax.ShapeDtypeStruct((batch, seq // tm, rows, tm), bf)
    return pl.pallas_call(
        _fox_proj_kernel,
        grid=(batch, seq // tm),
        in_specs=[
            row_spec(d),
            pl.BlockSpec((None, 1, 3 * d), lambda b, i: (b, 0, 0)),
            full((d, d)), full((d, d)), full((d, d)),
            full((d, LANES)), full((1, LANES)), full((1, LANES)),
        ],
        out_specs=[col_spec(d), row_spec(d), col_spec(d), col_spec(LANES), row_spec(LANES)],
        out_shape=[cols_of(d), rows_of(d), cols_of(d), cols_of(LANES), rows_of(LANES)],
        scratch_shapes=[pltpu.VMEM((CARRY_ROWS, LANES), jnp.float32)],
        compiler_params=_cparams(("arbitrary", "arbitrary")),
        name="fox_proj",
    )(x, mod, wqt, wk, wvt, w_f, bias, jnp.asarray(code))


def _rope(x, cos, sin_signed):
    return x * cos + pltpu.roll(x, MLA_ROPE, axis=1) * sin_signed


def _rope_t(xt, cos_t, sin_signed_t):
    partner = jnp.concatenate([xt[MLA_ROPE:, :], xt[:MLA_ROPE, :]], axis=0)
    return xt * cos_t + partner * sin_signed_t


def _mla_proj_kernel(x_ref, mod_ref, wa_ref, gq_ref, gkv_ref, wuqt_ref, wkn_ref, wvt_ref,
                     cos_ref, sin_ref, cost_ref, sint_ref, qt_ref, kn_ref, kr_ref, vt_ref):
    h = _mod_norm(x_ref[...], mod_ref).astype(jnp.bfloat16)
    a = _dot(h, wa_ref[...])
    kv0 = MLA_Q_RANK
    kr0 = MLA_Q_RANK + MLA_KV_RANK
    c_q = (_rms(a[:, 0:kv0]) * gq_ref[...]).astype(jnp.bfloat16)
    c_kv = (_rms(a[:, kv0:kr0]) * gkv_ref[...]).astype(jnp.bfloat16)
    kr_ref[...] = _rope(a[:, kr0:kr0 + LANES], cos_ref[...], sin_ref[...]).astype(kr_ref.dtype)
    kn_ref[...] = _dot(c_kv, wkn_ref[...]).astype(kn_ref.dtype)
    vt_ref[...] = _dot_nt(wvt_ref[...], c_kv).astype(vt_ref.dtype)
    scale = (MLA_NOPE + MLA_ROPE) ** -0.5 * LOG2E
    cos_t = cost_ref[...]
    sin_t = sint_ref[...]
    for p in range(N_PAIRS):
        r0 = p * 2 * LANES
        qp = _dot_nt(wuqt_ref[r0:r0 + 2 * LANES, :], c_q)
        qt_ref[r0:r0 + LANES, :] = (qp[0:LANES, :] * scale).astype(qt_ref.dtype)
        qt_ref[r0 + LANES:r0 + 2 * LANES, :] = (
            _rope_t(qp[LANES:2 * LANES, :], cos_t, sin_t) * scale).astype(qt_ref.dtype)


def _mla_weight_layout(w_a, w_uq, w_ukv):
    half = MLA_ROPE // 2
    kr0 = MLA_Q_RANK + MLA_KV_RANK
    x1 = np.arange(half)
    x2 = half + np.arange(half)
    kr_cols = kr0 + np.concatenate([x1, x1, x2, x2] * 2)
    wa_cols = np.concatenate([np.arange(kr0), kr_cols])
    q_cols = []
    qd = MLA_NOPE + MLA_ROPE
    for p in range(N_PAIRS):
        h0, h1 = 2 * p * qd, (2 * p + 1) * qd
        rope = np.concatenate([h0 + MLA_NOPE + x1, h1 + MLA_NOPE + x1,
                               h0 + MLA_NOPE + x2, h1 + MLA_NOPE + x2])
        q_cols.append(np.concatenate([h0 + np.arange(MLA_NOPE), h1 + np.arange(MLA_NOPE),
                                      rope, rope]))
    q_cols = np.concatenate(q_cols)
    kvd = MLA_NOPE + MLA_V
    kn_cols = np.concatenate([hh * kvd + np.arange(MLA_NOPE) for hh in range(N_HEADS)])
    v_cols = np.concatenate([hh * kvd + MLA_NOPE + np.arange(MLA_V) for hh in range(N_HEADS)])
    bf = jnp.bfloat16
    return (w_a[:, wa_cols].astype(bf), w_uq[:, q_cols].T.astype(bf),
            w_ukv[:, kn_cols].astype(bf), w_ukv[:, v_cols].T.astype(bf))


def _rope_tables(seq):
    half = MLA_ROPE // 2
    pos = jnp.arange(seq, dtype=jnp.float32)
    inv_freq = ROPE_BASE ** (-jnp.arange(0, MLA_ROPE, 2, dtype=jnp.float32) / MLA_ROPE)
    ang = pos[:, None] * inv_freq[None, :]
    cos = jnp.tile(jnp.cos(ang), (1, LANES // half))
    sign = np.where((np.arange(LANES) % (4 * half)) < 2 * half, -1.0, 1.0).astype(np.float32)
    sin = jnp.tile(jnp.sin(ang), (1, LANES // half)) * sign[None, :]
    return cos, sin


def _mla_projection(x, mod, w_a, g_q, g_kv, w_uq, w_ukv):
    batch, seq, _ = x.shape
    wa, wuqt, wkn, wvt = _mla_weight_layout(w_a, w_uq, w_ukv)
    cos, sin = _rope_tables(seq)
    tm = PROJ_TM
    d = D_MODEL
    row_spec = lambda width: pl.BlockSpec((None, tm, width), lambda b, i: (b, i, 0))
    col_spec = lambda rows: pl.BlockSpec((None, None, rows, tm), lambda b, i: (b, i, 0, 0))
    full = lambda shape: pl.BlockSpec(shape, lambda b, i: tuple(0 for _ in shape))
    tab = pl.BlockSpec((tm, LANES), lambda b, i: (i, 0))
    tab_t = pl.BlockSpec((LANES, tm), lambda b, i: (0, i))
    bf = jnp.bfloat16
    return pl.pallas_call(
        _mla_proj_kernel,
        grid=(batch, seq // tm),
        in_specs=[
            row_spec(d),
            pl.BlockSpec((None, 1, 3 * d), lambda b, i: (b, 0, 0)),
            full(wa.shape), full((1, MLA_Q_RANK)), full((1, MLA_KV_RANK)),
            full(wuqt.shape), full(wkn.shape), full(wvt.shape), tab, tab, tab_t, tab_t,
        ],
        out_specs=[col_spec(2 * d), row_spec(d), row_spec(LANES), col_spec(d)],
        out_shape=[jax.ShapeDtypeStruct((batch, seq // tm, 2 * d, tm), bf),
                   jax.ShapeDtypeStruct((batch, seq, d), bf),
                   jax.ShapeDtypeStruct((batch, seq, LANES), bf),
                   jax.ShapeDtypeStruct((batch, seq // tm, d, tm), bf)],
        compiler_params=_cparams(("arbitrary", "arbitrary")),
        name="mla_proj",
    )(x, mod, wa, g_q.reshape(1, -1), g_kv.reshape(1, -1), wuqt, wkn, wvt, cos, sin, cos.T, sin.T)


def _head_row_masks(pair, fox):
    row = lax.broadcasted_iota(jnp.int32, (2 * LANES, 1), 0)
    masks = []
    for hh in range(2):
        m = (row >= hh * HEAD_DIM) & (row < (hh + 1) * HEAD_DIM)
        r = row - LANES
        if fox:
            lo = (2 * pair + hh) * EXT_LANES
            m = m | ((r >= lo) & (r < lo + EXT_LANES))
        else:
            in_rope = (r >= 0) & (r < 2 * MLA_ROPE)
            m = m | (in_rope & (((r >> 4) & 1) == hh))
        masks.append(m)
    return masks


def _attn_kernel(*refs, fox):
    if fox:
        qt_ref, qxt_ref, k_ref, kx_ref, vt_ref, o_ref, m_ref, acc_ref, s_ref, cmax_ref = refs
    else:
        qt_ref, k_ref, kx_ref, vt_ref, o_ref, m_ref, acc_ref, s_ref, cmax_ref = refs
    pair = pl.program_id(1)
    qi = pl.program_id(2)
    tk = ATTN_TK
    tq = ATTN_TQ
    last = 2 * qi + 1

    qt = jnp.concatenate([qt_ref[0], qt_ref[1]], axis=1)
    if fox:
        qt = jnp.concatenate([qt, jnp.concatenate([qxt_ref[0], qxt_ref[1]], axis=1)], axis=0)
    q_heads = [jnp.where(m, qt, jnp.zeros_like(qt)) for m in _head_row_masks(pair, fox)]

    m_ref[...] = jnp.full(m_ref.shape, -jnp.inf, jnp.float32)
    acc_ref[...] = jnp.zeros_like(acc_ref)
    vrow = lax.broadcasted_iota(jnp.int32, (LANES, 1), 0)

    n_piece = 2
    qw = tq // n_piece

    def load_keys(j):
        start = pl.multiple_of(j * tk, tk)
        return jnp.concatenate([k_ref[pl.ds(start, tk), :], kx_ref[pl.ds(start, tk), :]], axis=1)

    def load_values(j):
        vt = vt_ref[j]
        ones = jnp.ones_like(vt)
        return [jnp.where(vrow < HEAD_DIM, vt, ones), jnp.where(vrow >= HEAD_DIM, vt, ones)]

    def visible(c, key_offset):
        if key_offset is None or c * qw >= key_offset + tk - 1:
            return "all"
        return "none" if (c + 1) * qw - 1 < key_offset else "some"

    def score_piece(k_t, slot, hh, c, key_offset=None, unmasked=False):
        cols = slice(c * qw, (c + 1) * qw)
        s = _dot(k_t, q_heads[hh][:, cols])
        if visible(c, key_offset) == "some":
            key = lax.broadcasted_iota(jnp.int32, (tk, qw), 0) + key_offset
            qry = lax.broadcasted_iota(jnp.int32, (tk, qw), 1) + c * qw
            s = jnp.where((qry >= key) | unmasked, s, -jnp.inf)
        s_ref[slot, hh, :, cols] = s
        cmax_ref[slot, hh, :, cols] = jnp.max(s, axis=0, keepdims=True)

    def pv_piece(vts, slot, hh, c):
        cols = slice(c * qw, (c + 1) * qw)
        m_old = m_ref[hh, :, cols]
        m_new = jnp.maximum(m_old, cmax_ref[slot, hh, :, cols])
        m_ref[hh, :, cols] = m_new
        alpha = jnp.exp2(m_old - m_new)
        p = jnp.exp2((s_ref[slot, hh, :, cols] - m_new).astype(jnp.bfloat16))
        acc_ref[hh, :, cols] = alpha * acc_ref[hh, :, cols] + _dot(vts[hh], p)

    def advance(j, slot, key_offset=None):
        k_t = load_keys(j + 1)
        vts = load_values(j)
        for hh in range(2):
            for c in range(n_piece):
                if visible(c, key_offset) != "none":
                    score_piece(k_t, 1 - slot, hh, c, key_offset)
                pv_piece(vts, slot, hh, c)

    k_t = load_keys(0)
    for hh in range(2):
        for c in range(n_piece):
            score_piece(k_t, 0, hh, c, key_offset=0, unmasked=qi > 0)

    def body(i, carry):
        advance(2 * i, 0)
        advance(2 * i + 1, 1)
        return carry

    lax.fori_loop(0, qi - 1, body, 0)

    @pl.when(qi > 0)
    def _():
        advance(last - 3, 0)
        advance(last - 2, 1, key_offset=0)

    advance(last - 1, 0, key_offset=tk)

    vts = load_values(last)
    for hh in range(2):
        for c in range(n_piece):
            if visible(c, tk) != "none":
                pv_piece(vts, 1, hh, c)

    a0 = acc_ref[0]
    a1 = acc_ref[1]
    first = vrow < HEAD_DIM
    num = jnp.where(first, a0, a1)
    den = jnp.where(first, a0[HEAD_DIM:HEAD_DIM + 1, :], a1[0:1, :])
    o_ref[...] = (num / den).T.astype(o_ref.dtype)


def _attention(qt, k, kx, vt, *, qxt=None):
    batch, seq, _ = k.shape
    tq, tk = ATTN_TQ, ATTN_TK
    assert tq == 2 * tk and vt.shape[1:] == (seq // tk, D_MODEL, tk) and qt.shape[3] == tk
    fox = qxt is not None
    rows = qt.shape[2] // N_PAIRS
    shared_rows = lambda rows: pl.BlockSpec((None, rows, LANES), lambda b, p, i: (b, 0, 0))
    in_specs = [pl.BlockSpec((None, 2, rows, tk), lambda b, p, i: (b, i, p, 0))]
    args = [qt]
    if fox:
        in_specs.append(pl.BlockSpec((None, 2, LANES, tk), lambda b, p, i: (b, i, 0, 0)))
        args.append(qxt)
    in_specs += [pl.BlockSpec((None, seq, LANES), lambda b, p, i: (b, 0, p)),
                 shared_rows(seq),
                 pl.BlockSpec((None, seq // tk, LANES, tk), lambda b, p, i: (b, 0, p, 0))]
    args += [k, kx, vt]
    return pl.pallas_call(
        functools.partial(_attn_kernel, fox=fox),
        grid=(batch, N_PAIRS, seq // tq),
        in_specs=in_specs,
        out_specs=pl.BlockSpec((None, tq, LANES), lambda b, p, i: (b, i, p)),
        out_shape=jax.ShapeDtypeStruct((batch, seq, D_MODEL), jnp.bfloat16),
        scratch_shapes=[pltpu.VMEM((2, 1, tq), jnp.float32),
                        pltpu.VMEM((2, LANES, tq), jnp.float32),
                        pltpu.VMEM((2, 2, tk, tq), jnp.float32),
                        pltpu.VMEM((2, 2, 1, tq), jnp.float32)],
        compiler_params=_cparams(("arbitrary", "arbitrary", "arbitrary")),
        name="attn_fox" if fox else "attn_mla",
    )(*args)


def _ffn_kernel(x_ref, o_ref, moda_ref, modf_ref, wo_ref, win_ref, cw_ref, cb_ref, wout_ref, fg_ref,
                out_ref, xn_ref, h_ref, acc_ref, u_ref, carry_ref, *, final_norm):
    i = pl.program_id(1)
    tm = x_ref.shape[0]
    fc = win_ref.shape[2]
    nf = wout_ref.shape[0]

    gate_a = moda_ref[:, 2 * D_MODEL:3 * D_MODEL]
    xn_ref[...] = x_ref[...] + gate_a * _dot(o_ref[...], wo_ref[...])
    h_ref[...] = _mod_norm(xn_ref[...], modf_ref).astype(h_ref.dtype)

    n_piece = FFN_ROW_PIECES
    rp = tm // n_piece

    def up(f, r):
        rows = slice(r * rp, (r + 1) * rp)
        for part in range(2):
            dst = u_ref.at[f % 2, part]
            if r == 0:
                prev = carry_ref[f, part]
                dst[0:CARRY_ROWS, :] = jnp.where(i == 0, jnp.zeros_like(prev), prev)
            dst[CARRY_ROWS + r * rp:CARRY_ROWS + (r + 1) * rp, :] = _dot(
                h_ref[rows, :], win_ref[part * nf + f])
            if r == n_piece - 1:
                carry_ref[f, part] = dst[tm:tm + CARRY_ROWS, :]

    def down(f, r):
        ys = []
        for part in range(2):
            src = u_ref.at[f % 2, part]
            c0 = part * D_FF + f * fc
            y = cb_ref[:, c0:c0 + fc]
            for tap in range(CONV_WIDTH):
                off = CARRY_ROWS - (CONV_WIDTH - 1) + tap + r * rp
                y = y + cw_ref[tap:tap + 1, c0:c0 + fc] * src[off:off + rp, :]
            ys.append(y)
        yg, yv = ys
        act = ((yg * (1.0 / (1.0 + jnp.exp(-yg)))) * yv).astype(jnp.bfloat16)
        rows = slice(r * rp, (r + 1) * rp)
        d = _dot(act, wout_ref[f])
        if f == 0:
            acc_ref[rows, :] = d
        else:
            acc_ref[rows, :] += d

    for r in range(n_piece):
        up(0, r)
    for f in range(nf):
        for r in range(n_piece):
            if f + 1 < nf:
                up(f + 1, r)
            down(f, r)

    gate_f = modf_ref[:, 2 * D_MODEL:3 * D_MODEL]
    y = xn_ref[...] + gate_f * acc_ref[...]
    if final_norm:
        y = _rms(y) * fg_ref[...]
    out_ref[...] = y


def _mixer_out_and_ffn(x, o, mod_a, mod_f, w_o, w_in, conv_w, conv_b, w_out, final_g, final_norm):
    batch, seq, _ = x.shape
    tm, fc = FFN_TM, FFN_FC
    nf = D_FF // fc
    bf = jnp.bfloat16
    d = D_MODEL
    w_in_c = w_in.astype(bf).reshape(d, 2 * nf, fc).transpose(1, 0, 2)
    w_out_c = w_out.astype(bf).reshape(nf, fc, d)
    row = lambda: pl.BlockSpec((None, tm, d), lambda b, i: (b, i, 0))
    mod_spec = lambda: pl.BlockSpec((None, 1, 3 * d), lambda b, i: (b, 0, 0))
    whole = lambda shape: pl.BlockSpec(shape, lambda b, i: tuple(0 for _ in shape),
                                       pipeline_mode=pl.Buffered(1))
    return pl.pallas_call(
        functools.partial(_ffn_kernel, final_norm=final_norm),
        grid=(batch, seq // tm),
        in_specs=[
            row(), row(), mod_spec(), mod_spec(),
            whole((d, d)), whole((2 * nf, d, fc)), whole((CONV_WIDTH, 2 * D_FF)),
            whole((1, 2 * D_FF)), whole((nf, fc, d)), whole((1, d)),
        ],
        out_specs=row(),
        out_shape=jax.ShapeDtypeStruct((batch, seq, d), jnp.float32),
        scratch_shapes=[
            pltpu.VMEM((tm, d), jnp.float32),
            pltpu.VMEM((tm, d), bf),
            pltpu.VMEM((tm, d), jnp.float32),
            pltpu.VMEM((2, 2, tm + CARRY_ROWS, fc), jnp.float32),
            pltpu.VMEM((nf, 2, CARRY_ROWS, fc), jnp.float32),
        ],
        compiler_params=_cparams(("arbitrary", "arbitrary")),
        name="mixer_out_ffn",
    )(x, o, mod_a, mod_f, w_o.astype(bf), w_in_c, conv_w, conv_b.reshape(1, -1), w_out_c,
      final_g.reshape(1, -1))


def kernel(x, c, ada_w, ada_b, fox_w_in, fox_b_f, fox_w_o, mla_w_a, mla_g_q, mla_g_kv, mla_w_uq,
           mla_w_ukv, mla_w_o, ffn_w_in, ffn_conv_w, ffn_conv_b, ffn_w_out, final_g):
    depth = ada_w.shape[0]
    mods = _ada_modulation(c, ada_w, ada_b)
    for layer in range(depth):
        j = layer // 2
        mod_a, mod_f = mods[2 * layer], mods[2 * layer + 1]
        if layer % 2 == 0:
            qt, k, vt, qxt, kx = _fox_projection(x, mod_a, fox_w_in[j], fox_b_f[j])
            o = _attention(qt, k, kx, vt, qxt=qxt)
            w_o = fox_w_o[j]
        else:
            qt, kn, kr, vt = _mla_projection(x, mod_a, mla_w_a[j], mla_g_q[j], mla_g_kv[j],
                                             mla_w_uq[j], mla_w_ukv[j])
            o = _attention(qt, kn, kr, vt)
            w_o = mla_w_o[j]
        x = _mixer_out_and_ffn(x, o, mod_a, mod_f, w_o, ffn_w_in[layer], ffn_conv_w[layer],
                               ffn_conv_b[layer], ffn_w_out[layer], final_g,
                               final_norm=(layer == depth - 1))
    return x
```

```python
import functools

import jax
import jax.numpy as jnp
import numpy as np
from jax import lax
from jax.experimental import pallas as pl
from jax.experimental.pallas import tpu as pltpu

D_MODEL = 1024
N_HEADS = 16
HEAD_DIM = 64
N_PAIRS = N_HEADS // 2
LANES = 128
NORM_EPS = 1e-6
MLA_NOPE = 64
MLA_ROPE = 32
MLA_V = 64
MLA_Q_RANK = 384
MLA_KV_RANK = 256
ROPE_BASE = 10000.0
D_FF = 2816
CONV_WIDTH = 3

VMEM_LIMIT_BYTES = 56 * 1024 * 1024

ADA_TN = 1536
PROJ_TM = 512
ATTN_TK = 512
ATTN_TQ = 2048
ATTN_QW = 512
FFN_TM = 512
FFN_FC = 256
FFN_ROW_PIECES = 4
CARRY_ROWS = 8
EXT_LANES = 6
LOG2E = 1.4426950408889634


def _cparams(semantics, flags=None):
    return pltpu.CompilerParams(dimension_semantics=semantics,
                                vmem_limit_bytes=VMEM_LIMIT_BYTES, flags=flags)


def _rms(x):
    return x * lax.rsqrt(jnp.mean(x * x, axis=-1, keepdims=True) + NORM_EPS)


def _mod_norm(x, mod_ref):
    shift = mod_ref[:, 0:D_MODEL]
    scale = mod_ref[:, D_MODEL:2 * D_MODEL]
    return _rms(x) * (1.0 + scale) + shift


def _dot(a, b):
    return jnp.dot(a, b, preferred_element_type=jnp.float32)


def _dot_nt(a, b):
    return lax.dot_general(a, b, (((1,), (1,)), ((), ())), preferred_element_type=jnp.float32)


def _ada_kernel(c_ref, w_ref, b_ref, o_ref):
    c = c_ref[...]
    sc = (c * (1.0 / (1.0 + jnp.exp(-c)))).astype(jnp.bfloat16)
    o_ref[...] = _dot(sc, w_ref[...].astype(jnp.bfloat16)) + b_ref[...]


def _ada_modulation(c, ada_w, ada_b):
    n_mod = ada_w.shape[0] * ada_w.shape[1]
    batch = c.shape[0]
    rows = 8
    c_pad = jnp.zeros((rows, D_MODEL), jnp.float32).at[:batch].set(c)
    w = ada_w.reshape(n_mod, D_MODEL, 3 * D_MODEL)
    b = ada_b.reshape(n_mod, 1, 3 * D_MODEL)
    out = pl.pallas_call(
        _ada_kernel,
        grid=(n_mod, 3 * D_MODEL // ADA_TN),
        in_specs=[
            pl.BlockSpec((rows, D_MODEL), lambda l, n: (0, 0)),
            pl.BlockSpec((None, D_MODEL, ADA_TN), lambda l, n: (l, 0, n)),
            pl.BlockSpec((None, 1, ADA_TN), lambda l, n: (l, 0, n)),
        ],
        out_specs=pl.BlockSpec((None, rows, ADA_TN), lambda l, n: (l, 0, n)),
        out_shape=jax.ShapeDtypeStruct((n_mod, rows, 3 * D_MODEL), jnp.float32),
        compiler_params=_cparams(("arbitrary", "arbitrary")),
        name="ada_mod",
    )(c_pad, w, b)
    return out[:, :batch].reshape(n_mod, batch, 1, 3 * D_MODEL)


def _split3(x):
    hi = x.astype(jnp.bfloat16).astype(jnp.float32)
    r = x - hi
    mid = r.astype(jnp.bfloat16).astype(jnp.float32)
    lo = (r - mid).astype(jnp.bfloat16).astype(jnp.float32)
    return hi, mid, lo


def _fox_proj_kernel(x_ref, mod_ref, wqt_ref, wk_ref, wvt_ref, wf_ref, bf_ref, code_ref,
                     qt_ref, k_ref, vt_ref, qxt_ref, kx_ref, carry_ref):
    i = pl.program_id(1)
    tm = x_ref.shape[0]
    h = _mod_norm(x_ref[...], mod_ref).astype(jnp.bfloat16)
    qt_ref[...] = (_dot_nt(wqt_ref[...], h) * (HEAD_DIM ** -0.5 * LOG2E)).astype(qt_ref.dtype)
    k_ref[...] = _dot(h, wk_ref[...]).astype(k_ref.dtype)
    vt_ref[...] = _dot_nt(wvt_ref[...], h).astype(vt_ref.dtype)

    z = _dot(h, wf_ref[...]) + bf_ref[...]
    log_f = jnp.minimum(z, 0.0) - jnp.log(1.0 + jnp.exp(-jnp.abs(z)))

    @pl.when(i == 0)
    def _():
        carry_ref[...] = jnp.zeros_like(carry_ref)

    row = lax.broadcasted_iota(jnp.int32, (tm, tm), 0)
    col = lax.broadcasted_iota(jnp.int32, (tm, tm), 1)
    tri = (row >= col).astype(jnp.bfloat16)
    bf = jnp.bfloat16
    hi, mid, lo = _split3(log_f)
    cum = (_dot(tri, hi.astype(bf)) + _dot(tri, mid.astype(bf))) + _dot(tri, lo.astype(bf))
    cum = cum + carry_ref[0:1, :]
    carry_ref[...] = jnp.broadcast_to(cum[tm - 1:tm, :], carry_ref.shape)

    hi, mid, lo = _split3(cum * LOG2E)
    code = code_ref[...]
    one = jnp.ones_like(hi)
    zero = jnp.zeros_like(hi)
    qx = jnp.where(code == 0, hi, jnp.where(code == 1, mid, jnp.where(code == 2, lo,
                   jnp.where(code < EXT_LANES, one, zero))))
    kx = jnp.where(code < 3, one, jnp.where(code == 3, -hi, jnp.where(code == 4, -mid,
                   jnp.where(code == 5, -lo, zero))))
    qxt_ref[...] = qx.T.astype(qxt_ref.dtype)
    kx_ref[...] = kx.astype(kx_ref.dtype)


def _fox_projection(x, mod, w_in, b_f):
    batch, seq, _ = x.shape
    bf = jnp.bfloat16
    d = D_MODEL
    wqt = w_in[:, 0:d].T.astype(bf)
    wk = w_in[:, d:2 * d].astype(bf)
    wvt = w_in[:, 2 * d:3 * d].T.astype(bf)
    lane = np.arange(LANES)
    used = lane < EXT_LANES * N_HEADS
    head_of_lane = np.where(used, lane // EXT_LANES, 0)
    code = np.where(used, lane % EXT_LANES, EXT_LANES).astype(np.int32)[None, :]
    w_f = jnp.where(used[None, :], w_in[:, 3 * d:][:, head_of_lane], 0.0).astype(bf)
    bias = jnp.where(used, b_f[head_of_lane], 0.0)[None, :]
    tm = PROJ_TM
    row_spec = lambda width: pl.BlockSpec((None, tm, width), lambda b, i: (b, i, 0))
    col_spec = lambda rows: pl.BlockSpec((None, None, rows, tm), lambda b, i: (b, i, 0, 0))
    full = lambda shape: pl.BlockSpec(shape, lambda b, i: tuple(0 for _ in shape))
    rows_of = lambda width: jax.ShapeDtypeStruct((batch, seq, width), bf)
    cols_of = lambda rows: jax.ShapeDtypeStruct((batch, seq // tm, rows, tm), bf)
    return pl.pallas_call(
        _fox_proj_kernel,
        grid=(batch, seq // tm),
        in_specs=[
            row_spec(d),
            pl.BlockSpec((None, 1, 3 * d), lambda b, i: (b, 0, 0)),
            full((d, d)), full((d, d)), full((d, d)),
            full((d, LANES)), full((1, LANES)), full((1, LANES)),
        ],
        out_specs=[col_spec(d), row_spec(d), col_spec(d), col_spec(LANES), row_spec(LANES)],
        out_shape=[cols_of(d), rows_of(d), cols_of(d), cols_of(LANES), rows_of(LANES)],
        scratch_shapes=[pltpu.VMEM((CARRY_ROWS, LANES), jnp.float32)],
        compiler_params=_cparams(("arbitrary", "arbitrary")),
        name="fox_proj",
    )(x, mod, wqt, wk, wvt, w_f, bias, jnp.asarray(code))


def _rope(x, cos, sin_signed):
    return x * cos + pltpu.roll(x, MLA_ROPE, axis=1) * sin_signed


def _rope_t(xt, cos_t, sin_signed_t):
    partner = jnp.concatenate([xt[MLA_ROPE:, :], xt[:MLA_ROPE, :]], axis=0)
    return xt * cos_t + partner * sin_signed_t


def _mla_proj_kernel(x_ref, mod_ref, wa_ref, gq_ref, gkv_ref, wuqt_ref, wkn_ref, wvt_ref,
                     cos_ref, sin_ref, cost_ref, sint_ref, qt_ref, kn_ref, kr_ref, vt_ref):
    h = _mod_norm(x_ref[...], mod_ref).astype(jnp.bfloat16)
    a = _dot(h, wa_ref[...])
    kv0 = MLA_Q_RANK
    kr0 = MLA_Q_RANK + MLA_KV_RANK
    c_q = (_rms(a[:, 0:kv0]) * gq_ref[...]).astype(jnp.bfloat16)
    c_kv = (_rms(a[:, kv0:kr0]) * gkv_ref[...]).astype(jnp.bfloat16)
    kr_ref[...] = _rope(a[:, kr0:kr0 + LANES], cos_ref[...], sin_ref[...]).astype(kr_ref.dtype)
    kn_ref[...] = _dot(c_kv, wkn_ref[...]).astype(kn_ref.dtype)
    vt_ref[...] = _dot_nt(wvt_ref[...], c_kv).astype(vt_ref.dtype)
    scale = (MLA_NOPE + MLA_ROPE) ** -0.5 * LOG2E
    cos_t = cost_ref[...]
    sin_t = sint_ref[...]
    for p in range(N_PAIRS):
        r0 = p * 2 * LANES
        qp = _dot_nt(wuqt_ref[r0:r0 + 2 * LANES, :], c_q)
        qt_ref[r0:r0 + LANES, :] = (qp[0:LANES, :] * scale).astype(qt_ref.dtype)
        qt_ref[r0 + LANES:r0 + 2 * LANES, :] = (
            _rope_t(qp[LANES:2 * LANES, :], cos_t, sin_t) * scale).astype(qt_ref.dtype)


def _mla_weight_layout(w_a, w_uq, w_ukv):
    half = MLA_ROPE // 2
    kr0 = MLA_Q_RANK + MLA_KV_RANK
    x1 = np.arange(half)
    x2 = half + np.arange(half)
    kr_cols = kr0 + np.concatenate([x1, x1, x2, x2] * 2)
    wa_cols = np.concatenate([np.arange(kr0), kr_cols])
    q_cols = []
    qd = MLA_NOPE + MLA_ROPE
    for p in range(N_PAIRS):
        h0, h1 = 2 * p * qd, (2 * p + 1) * qd
        rope = np.concatenate([h0 + MLA_NOPE + x1, h1 + MLA_NOPE + x1,
                               h0 + MLA_NOPE + x2, h1 + MLA_NOPE + x2])
        q_cols.append(np.concatenate([h0 + np.arange(MLA_NOPE), h1 + np.arange(MLA_NOPE),
                                      rope, rope]))
    q_cols = np.concatenate(q_cols)
    kvd = MLA_NOPE + MLA_V
    kn_cols = np.concatenate([hh * kvd + np.arange(MLA_NOPE) for hh in range(N_HEADS)])
    v_cols = np.concatenate([hh * kvd + MLA_NOPE + np.arange(MLA_V) for hh in range(N_HEADS)])
    bf = jnp.bfloat16
    return (w_a[:, wa_cols].astype(bf), w_uq[:, q_cols].T.astype(bf),
            w_ukv[:, kn_cols].astype(bf), w_ukv[:, v_cols].T.astype(bf))


def _rope_tables(seq):
    half = MLA_ROPE // 2
    pos = jnp.arange(seq, dtype=jnp.float32)
    inv_freq = ROPE_BASE ** (-jnp.arange(0, MLA_ROPE, 2, dtype=jnp.float32) / MLA_ROPE)
    ang = pos[:, None] * inv_freq[None, :]
    cos = jnp.tile(jnp.cos(ang), (1, LANES // half))
    sign = np.where((np.arange(LANES) % (4 * half)) < 2 * half, -1.0, 1.0).astype(np.float32)
    sin = jnp.tile(jnp.sin(ang), (1, LANES // half)) * sign[None, :]
    return cos, sin


def _mla_projection(x, mod, w_a, g_q, g_kv, w_uq, w_ukv):
    batch, seq, _ = x.shape
    wa, wuqt, wkn, wvt = _mla_weight_layout(w_a, w_uq, w_ukv)
    cos, sin = _rope_tables(seq)
    tm = PROJ_TM
    d = D_MODEL
    row_spec = lambda width: pl.BlockSpec((None, tm, width), lambda b, i: (b, i, 0))
    col_spec = lambda rows: pl.BlockSpec((None, None, rows, tm), lambda b, i: (b, i, 0, 0))
    full = lambda shape: pl.BlockSpec(shape, lambda b, i: tuple(0 for _ in shape))
    tab = pl.BlockSpec((tm, LANES), lambda b, i: (i, 0))
    tab_t = pl.BlockSpec((LANES, tm), lambda b, i: (0, i))
    bf = jnp.bfloat16
    return pl.pallas_call(
        _mla_proj_kernel,
        grid=(batch, seq // tm),
        in_specs=[
            row_spec(d),
            pl.BlockSpec((None, 1, 3 * d), lambda b, i: (b, 0, 0)),
            full(wa.shape), full((1, MLA_Q_RANK)), full((1, MLA_KV_RANK)),
            full(wuqt.shape), full(wkn.shape), full(wvt.shape), tab, tab, tab_t, tab_t,
        ],
        out_specs=[col_spec(2 * d), row_spec(d), row_spec(LANES), col_spec(d)],
        out_shape=[jax.ShapeDtypeStruct((batch, seq // tm, 2 * d, tm), bf),
                   jax.ShapeDtypeStruct((batch, seq, d), bf),
                   jax.ShapeDtypeStruct((batch, seq, LANES), bf),
                   jax.ShapeDtypeStruct((batch, seq // tm, d, tm), bf)],
        compiler_params=_cparams(("arbitrary", "arbitrary")),
        name="mla_proj",
    )(x, mod, wa, g_q.reshape(1, -1), g_kv.reshape(1, -1), wuqt, wkn, wvt, cos, sin, cos.T, sin.T)


def _head_row_masks(pair, fox):
    row = lax.broadcasted_iota(jnp.int32, (2 * LANES, 1), 0)
    masks = []
    for hh in range(2):
        m = (row >= hh * HEAD_DIM) & (row < (hh + 1) * HEAD_DIM)
        r = row - LANES
        if fox:
            lo = (2 * pair + hh) * EXT_LANES
            m = m | ((r >= lo) & (r < lo + EXT_LANES))
        else:
            in_rope = (r >= 0) & (r < 2 * MLA_ROPE)
            m = m | (in_rope & (((r >> 4) & 1) == hh))
        masks.append(m)
    return masks


def _attn_kernel(*refs, fox):
    if fox:
        qt_ref, qxt_ref, k_ref, kx_ref, vt_ref, o_ref, m_ref, acc_ref, s_ref, cmax_ref = refs
    else:
        qt_ref, k_ref, kx_ref, vt_ref, o_ref, m_ref, acc_ref, s_ref, cmax_ref = refs
    pair = pl.program_id(1)
    qi = pl.program_id(2)
    tk = ATTN_TK
    tq = ATTN_TQ
    nt = tq // tk
    last = nt * qi + nt - 1

    qt = jnp.concatenate([qt_ref[t] for t in range(nt)], axis=1)
    if fox:
        ext = jnp.concatenate([qxt_ref[t] for t in range(nt)], axis=1)
        qt = jnp.concatenate([qt, ext], axis=0)
    q_heads = [jnp.where(m, qt, jnp.zeros_like(qt)) for m in _head_row_masks(pair, fox)]

    m_ref[...] = jnp.full(m_ref.shape, -jnp.inf, jnp.float32)
    acc_ref[...] = jnp.zeros_like(acc_ref)
    vrow = lax.broadcasted_iota(jnp.int32, (LANES, 1), 0)

    qw = ATTN_QW
    n_piece = tq // qw

    def load_keys(j):
        start = pl.multiple_of(j * tk, tk)
        return jnp.concatenate([k_ref[pl.ds(start, tk), :], kx_ref[pl.ds(start, tk), :]], axis=1)

    def load_values(j):
        vt = vt_ref[j]
        ones = jnp.ones_like(vt)
        return [jnp.where(vrow < HEAD_DIM, vt, ones), jnp.where(vrow >= HEAD_DIM, vt, ones)]

    def visible(c, key_offset):
        if key_offset is None or c * qw >= key_offset + tk - 1:
            return "all"
        return "none" if (c + 1) * qw - 1 < key_offset else "some"

    def score_piece(k_t, slot, hh, c, key_offset=None, unmasked=False):
        cols = slice(c * qw, (c + 1) * qw)
        s = _dot(k_t, q_heads[hh][:, cols])
        if visible(c, key_offset) == "some":
            key = lax.broadcasted_iota(jnp.int32, (tk, qw), 0) + key_offset
            qry = lax.broadcasted_iota(jnp.int32, (tk, qw), 1) + c * qw
            s = jnp.where((qry >= key) | unmasked, s, -jnp.inf)
        s_ref[slot, hh, :, cols] = s
        cmax_ref[slot, hh, :, cols] = jnp.max(s, axis=0, keepdims=True)

    def pv_piece(vts, slot, hh, c):
        cols = slice(c * qw, (c + 1) * qw)
        m_old = m_ref[hh, :, cols]
        m_new = jnp.maximum(m_old, cmax_ref[slot, hh, :, cols])
        m_ref[hh, :, cols] = m_new
        alpha = jnp.exp2(m_old - m_new)
        p = jnp.exp2((s_ref[slot, hh, :, cols] - m_new).astype(jnp.bfloat16))
        acc_ref[hh, :, cols] = alpha * acc_ref[hh, :, cols] + _dot(vts[hh], p)

    def advance(j, slot, key_offset=None, next_offset=None):
        k_t = load_keys(j + 1)
        vts = load_values(j)
        for hh in range(2):
            for c in range(n_piece):
                if visible(c, next_offset) != "none":
                    score_piece(k_t, 1 - slot, hh, c, next_offset)
                if visible(c, key_offset) != "none":
                    pv_piece(vts, slot, hh, c)

    k_t = load_keys(0)
    for hh in range(2):
        for c in range(n_piece):
            score_piece(k_t, 0, hh, c, key_offset=0, unmasked=qi > 0)

    def body(i, carry):
        advance(2 * i, 0)
        advance(2 * i + 1, 1)
        return carry

    lax.fori_loop(0, (nt // 2) * qi - 1, body, 0)

    @pl.when(qi > 0)
    def _():
        advance(last - nt - 1, 0)
        advance(last - nt, 1, next_offset=0)

    for t in range(1, nt):
        advance(last - nt + t, (t - 1) % 2, key_offset=(t - 1) * tk, next_offset=t * tk)

    vts = load_values(last)
    for hh in range(2):
        for c in range(n_piece):
            if visible(c, (nt - 1) * tk) != "none":
                pv_piece(vts, (nt - 1) % 2, hh, c)

    a0 = acc_ref[0]
    a1 = acc_ref[1]
    first = vrow < HEAD_DIM
    num = jnp.where(first, a0, a1)
    den = jnp.where(first, a0[HEAD_DIM:HEAD_DIM + 1, :], a1[0:1, :])
    o_ref[...] = (num / den).T.astype(o_ref.dtype)


def _attention(qt, k, kx, vt, *, qxt=None):
    batch, seq, _ = k.shape
    tq, tk = ATTN_TQ, ATTN_TK
    nt = tq // tk
    assert tq == nt * tk and nt % 2 == 0 and tq % ATTN_QW == 0
    assert vt.shape[1:] == (seq // tk, D_MODEL, tk) and qt.shape[3] == tk
    fox = qxt is not None
    rows = qt.shape[2] // N_PAIRS
    shared_rows = lambda rows: pl.BlockSpec((None, rows, LANES), lambda b, p, i: (b, 0, 0))
    in_specs = [pl.BlockSpec((None, nt, rows, tk), lambda b, p, i: (b, i, p, 0))]
    args = [qt]
    if fox:
        in_specs.append(pl.BlockSpec((None, nt, LANES, tk), lambda b, p, i: (b, i, 0, 0)))
        args.append(qxt)
    in_specs += [pl.BlockSpec((None, seq, LANES), lambda b, p, i: (b, 0, p)),
                 shared_rows(seq),
                 pl.BlockSpec((None, seq // tk, LANES, tk), lambda b, p, i: (b, 0, p, 0))]
    args += [k, kx, vt]
    return pl.pallas_call(
        functools.partial(_attn_kernel, fox=fox),
        grid=(batch, N_PAIRS, seq // tq),
        in_specs=in_specs,
        out_specs=pl.BlockSpec((None, tq, LANES), lambda b, p, i: (b, i, p)),
        out_shape=jax.ShapeDtypeStruct((batch, seq, D_MODEL), jnp.bfloat16),
        scratch_shapes=[pltpu.VMEM((2, 1, tq), jnp.float32),
                        pltpu.VMEM((2, LANES, tq), jnp.float32),
                        pltpu.VMEM((2, 2, tk, tq), jnp.float32),
                        pltpu.VMEM((2, 2, 1, tq), jnp.float32)],
        compiler_params=_cparams(("arbitrary", "arbitrary", "arbitrary")),
        name="attn_fox" if fox else "attn_mla",
    )(*args)


def _ffn_kernel(x_ref, o_ref, moda_ref, modf_ref, wo_ref, win_ref, cw_ref, cb_ref, wout_ref, fg_ref,
                out_ref, xn_ref, h_ref, acc_ref, u_ref, carry_ref, *, final_norm):
    i = pl.program_id(1)
    tm = x_ref.shape[0]
    fc = wout_ref.shape[1]
    nf = wout_ref.shape[0]

    gate_a = moda_ref[:, 2 * D_MODEL:3 * D_MODEL]
    xn_ref[...] = x_ref[...] + gate_a * _dot(o_ref[...], wo_ref[...])
    h_ref[...] = _mod_norm(xn_ref[...], modf_ref).astype(h_ref.dtype)

    n_piece = FFN_ROW_PIECES
    rp = tm // n_piece

    def up(f, r):
        rows = slice(r * rp, (r + 1) * rp)
        for part in range(2):
            dst = u_ref.at[f % 2, part]
            if r == 0:
                prev = carry_ref[f, part]
                dst[0:CARRY_ROWS, :] = jnp.where(i == 0, jnp.zeros_like(prev), prev)
            c0 = part * D_FF + f * fc
            dst[CARRY_ROWS + r * rp:CARRY_ROWS + (r + 1) * rp, :] = _dot(
                h_ref[rows, :], win_ref[:, c0:c0 + fc])
            if r == n_piece - 1:
                carry_ref[f, part] = dst[tm:tm + CARRY_ROWS, :]

    def down(f, r):
        ys = []
        for part in range(2):
            src = u_ref.at[f % 2, part]
            c0 = part * D_FF + f * fc
            y = cb_ref[:, c0:c0 + fc]
            for tap in range(CONV_WIDTH):
                off = CARRY_ROWS - (CONV_WIDTH - 1) + tap + r * rp
                y = y + cw_ref[tap:tap + 1, c0:c0 + fc] * src[off:off + rp, :]
            ys.append(y)
        yg, yv = ys
        act = ((yg * (1.0 / (1.0 + jnp.exp(-yg)))) * yv).astype(jnp.bfloat16)
        rows = slice(r * rp, (r + 1) * rp)
        d = _dot(act, wout_ref[f])
        if f == 0:
            acc_ref[rows, :] = d
        else:
            acc_ref[rows, :] += d

    for r in range(n_piece):
        up(0, r)
    for f in range(nf):
        for r in range(n_piece):
            if f + 1 < nf:
                up(f + 1, r)
            down(f, r)

    gate_f = modf_ref[:, 2 * D_MODEL:3 * D_MODEL]
    y = xn_ref[...] + gate_f * acc_ref[...]
    if final_norm:
        y = _rms(y) * fg_ref[...]
    out_ref[...] = y


def _mixer_out_and_ffn(x, o, mod_a, mod_f, w_o, w_in, conv_w, conv_b, w_out, final_g, final_norm):
    batch, seq, _ = x.shape
    tm, fc = FFN_TM, FFN_FC
    nf = D_FF // fc
    bf = jnp.bfloat16
    d = D_MODEL
    w_out_c = w_out.astype(bf).reshape(nf, fc, d)
    row = lambda: pl.BlockSpec((None, tm, d), lambda b, i: (b, i, 0))
    mod_spec = lambda: pl.BlockSpec((None, 1, 3 * d), lambda b, i: (b, 0, 0))
    whole = lambda shape: pl.BlockSpec(shape, lambda b, i: tuple(0 for _ in shape),
                                       pipeline_mode=pl.Buffered(1))
    return pl.pallas_call(
        functools.partial(_ffn_kernel, final_norm=final_norm),
        grid=(batch, seq // tm),
        in_specs=[
            row(), row(), mod_spec(), mod_spec(),
            whole((d, d)), whole((d, 2 * D_FF)), whole((CONV_WIDTH, 2 * D_FF)),
            whole((1, 2 * D_FF)), whole((nf, fc, d)), whole((1, d)),
        ],
        out_specs=row(),
        out_shape=jax.ShapeDtypeStruct((batch, seq, d), jnp.float32),
        scratch_shapes=[
            pltpu.VMEM((tm, d), jnp.float32),
            pltpu.VMEM((tm, d), bf),
            pltpu.VMEM((tm, d), jnp.float32),
            pltpu.VMEM((2, 2, tm + CARRY_ROWS, fc), jnp.float32),
            pltpu.VMEM((nf, 2, CARRY_ROWS, fc), jnp.float32),
        ],
        compiler_params=_cparams(("arbitrary", "arbitrary")),
        name="mixer_out_ffn",
    )(x, o, mod_a, mod_f, w_o.astype(bf), w_in.astype(bf), conv_w, conv_b.reshape(1, -1), w_out_c,
      final_g.reshape(1, -1))


def kernel(x, c, ada_w, ada_b, fox_w_in, fox_b_f, fox_w_o, mla_w_a, mla_g_q, mla_g_kv, mla_w_uq,
           mla_w_ukv, mla_w_o, ffn_w_in, ffn_conv_w, ffn_conv_b, ffn_w_out, final_g):
    depth = ada_w.shape[0]
    mods = _ada_modulation(c, ada_w, ada_b)
    for layer in range(depth):
        j = layer // 2
        mod_a, mod_f = mods[2 * layer], mods[2 * layer + 1]
        if layer % 2 == 0:
            qt, k, vt, qxt, kx = _fox_projection(x, mod_a, fox_w_in[j], fox_b_f[j])
            o = _attention(qt, k, kx, vt, qxt=qxt)
            w_o = fox_w_o[j]
        else:
            qt, kn, kr, vt = _mla_projection(x, mod_a, mla_w_a[j], mla_g_q[j], mla_g_kv[j],
                                             mla_w_uq[j], mla_w_ukv[j])
            o = _attention(qt, kn, kr, vt)
            w_o = mla_w_o[j]
        x = _mixer_out_and_ffn(x, o, mod_a, mod_f, w_o, ffn_w_in[layer], ffn_conv_w[layer],
                               ffn_conv_b[layer], ffn_w_out[layer], final_g,
                               final_norm=(layer == depth - 1))
    return x
```

```python
import functools

import jax
import jax.numpy as jnp
import numpy as np
from jax import lax
from jax.experimental import pallas as pl
from jax.experimental.pallas import tpu as pltpu

D_MODEL = 1024
N_HEADS = 16
HEAD_DIM = 64
N_PAIRS = N_HEADS // 2
LANES = 128
NORM_EPS = 1e-6
MLA_NOPE = 64
MLA_ROPE = 32
MLA_V = 64
MLA_Q_RANK = 384
MLA_KV_RANK = 256
ROPE_BASE = 10000.0
D_FF = 2816
CONV_WIDTH = 3

VMEM_LIMIT_BYTES = 56 * 1024 * 1024

ADA_TN = 1536
PROJ_TM = 512
ATTN_TK = 512
ATTN_TQ = 2048
ATTN_QW = 512
FFN_TM = 512
FFN_FC = 256
FFN_ROW_PIECES = 4
CARRY_ROWS = 8
EXT_LANES = 6
LOG2E = 1.4426950408889634


def _cparams(semantics):
    return pltpu.CompilerParams(dimension_semantics=semantics,
                                vmem_limit_bytes=VMEM_LIMIT_BYTES)


def _rms(x):
    return x * lax.rsqrt(jnp.mean(x * x, axis=-1, keepdims=True) + NORM_EPS)


def _mod_norm(x, mod_ref):
    shift = mod_ref[:, 0:D_MODEL]
    scale = mod_ref[:, D_MODEL:2 * D_MODEL]
    return _rms(x) * (1.0 + scale) + shift


def _dot(a, b):
    return jnp.dot(a, b, preferred_element_type=jnp.float32)


def _dot_nt(a, b):
    return lax.dot_general(a, b, (((1,), (1,)), ((), ())), preferred_element_type=jnp.float32)


def _ada_kernel(c_ref, w_ref, b_ref, o_ref):
    c = c_ref[...]
    sc = (c * (1.0 / (1.0 + jnp.exp(-c)))).astype(jnp.bfloat16)
    o_ref[...] = _dot(sc, w_ref[...].astype(jnp.bfloat16)) + b_ref[...]


def _ada_modulation(c, ada_w, ada_b):
    n_mod = ada_w.shape[0] * ada_w.shape[1]
    batch = c.shape[0]
    rows = 8
    c_pad = jnp.zeros((rows, D_MODEL), jnp.float32).at[:batch].set(c)
    w = ada_w.reshape(n_mod, D_MODEL, 3 * D_MODEL)
    b = ada_b.reshape(n_mod, 1, 3 * D_MODEL)
    out = pl.pallas_call(
        _ada_kernel,
        grid=(n_mod, 3 * D_MODEL // ADA_TN),
        in_specs=[
            pl.BlockSpec((rows, D_MODEL), lambda l, n: (0, 0)),
            pl.BlockSpec((None, D_MODEL, ADA_TN), lambda l, n: (l, 0, n)),
            pl.BlockSpec((None, 1, ADA_TN), lambda l, n: (l, 0, n)),
        ],
        out_specs=pl.BlockSpec((None, rows, ADA_TN), lambda l, n: (l, 0, n)),
        out_shape=jax.ShapeDtypeStruct((n_mod, rows, 3 * D_MODEL), jnp.float32),
        compiler_params=_cparams(("arbitrary", "arbitrary")),
        name="ada_mod",
    )(c_pad, w, b)
    return out[:, :batch].reshape(n_mod, batch, 1, 3 * D_MODEL)


def _split3(x):
    hi = x.astype(jnp.bfloat16).astype(jnp.float32)
    r = x - hi
    mid = r.astype(jnp.bfloat16).astype(jnp.float32)
    lo = (r - mid).astype(jnp.bfloat16).astype(jnp.float32)
    return hi, mid, lo


def _fox_proj_kernel(x_ref, mod_ref, wqt_ref, wk_ref, wvt_ref, wf_ref, bf_ref, code_ref,
                     qt_ref, k_ref, vt_ref, qxt_ref, kx_ref, carry_ref):
    i = pl.program_id(1)
    tm = x_ref.shape[0]
    h = _mod_norm(x_ref[...], mod_ref).astype(jnp.bfloat16)
    qt_ref[...] = (_dot_nt(wqt_ref[...], h) * (HEAD_DIM ** -0.5 * LOG2E)).astype(qt_ref.dtype)
    k_ref[...] = _dot(h, wk_ref[...]).astype(k_ref.dtype)
    vt_ref[...] = _dot_nt(wvt_ref[...], h).astype(vt_ref.dtype)

    z = _dot(h, wf_ref[...]) + bf_ref[...]
    log_f = jnp.minimum(z, 0.0) - jnp.log(1.0 + jnp.exp(-jnp.abs(z)))

    @pl.when(i == 0)
    def _():
        carry_ref[...] = jnp.zeros_like(carry_ref)

    row = lax.broadcasted_iota(jnp.int32, (tm, tm), 0)
    col = lax.broadcasted_iota(jnp.int32, (tm, tm), 1)
    tri = (row >= col).astype(jnp.bfloat16)
    bf = jnp.bfloat16
    parts = jnp.concatenate([part.astype(bf) for part in _split3(log_f)], axis=1)
    sums = _dot(tri, parts)
    cum = (sums[:, 0:LANES] + sums[:, LANES:2 * LANES]) + sums[:, 2 * LANES:3 * LANES]
    cum = cum + carry_ref[0:1, :]
    carry_ref[...] = jnp.broadcast_to(cum[tm - 1:tm, :], carry_ref.shape)

    hi, mid, lo = _split3(cum * LOG2E)
    code = code_ref[...]
    one = jnp.ones_like(hi)
    zero = jnp.zeros_like(hi)
    qx = jnp.where(code == 0, hi, jnp.where(code == 1, mid, jnp.where(code == 2, lo,
                   jnp.where(code < EXT_LANES, one, zero))))
    kx = jnp.where(code < 3, one, jnp.where(code == 3, -hi, jnp.where(code == 4, -mid,
                   jnp.where(code == 5, -lo, zero))))
    qxt_ref[...] = qx.T.astype(qxt_ref.dtype)
    kx_ref[...] = kx.astype(kx_ref.dtype)


def _fox_projection(x, mod, w_in, b_f):
    batch, seq, _ = x.shape
    bf = jnp.bfloat16
    d = D_MODEL
    wqt = w_in[:, 0:d].T.astype(bf)
    wk = w_in[:, d:2 * d].astype(bf)
    wvt = w_in[:, 2 * d:3 * d].T.astype(bf)
    lane = np.arange(LANES)
    used = lane < EXT_LANES * N_HEADS
    head_of_lane = np.where(used, lane // EXT_LANES, 0)
    code = np.where(used, lane % EXT_LANES, EXT_LANES).astype(np.int32)[None, :]
    w_f = jnp.where(used[None, :], w_in[:, 3 * d:][:, head_of_lane], 0.0).astype(bf)
    bias = jnp.where(used, b_f[head_of_lane], 0.0)[None, :]
    tm = PROJ_TM
    row_spec = lambda width: pl.BlockSpec((None, tm, width), lambda b, i: (b, i, 0))
    col_spec = lambda rows: pl.BlockSpec((None, None, rows, tm), lambda b, i: (b, i, 0, 0))
    full = lambda shape: pl.BlockSpec(shape, lambda b, i: tuple(0 for _ in shape))
    rows_of = lambda width: jax.ShapeDtypeStruct((batch, seq, width), bf)
    cols_of = lambda rows: jax.ShapeDtypeStruct((batch, seq // tm, rows, tm), bf)
    return pl.pallas_call(
        _fox_proj_kernel,
        grid=(batch, seq // tm),
        in_specs=[
            row_spec(d),
            pl.BlockSpec((None, 1, 3 * d), lambda b, i: (b, 0, 0)),
            full((d, d)), full((d, d)), full((d, d)),
            full((d, LANES)), full((1, LANES)), full((1, LANES)),
        ],
        out_specs=[col_spec(d), row_spec(d), col_spec(d), col_spec(LANES), row_spec(LANES)],
        out_shape=[cols_of(d), rows_of(d), cols_of(d), cols_of(LANES), rows_of(LANES)],
        scratch_shapes=[pltpu.VMEM((CARRY_ROWS, LANES), jnp.float32)],
        compiler_params=_cparams(("arbitrary", "arbitrary")),
        name="fox_proj",
    )(x, mod, wqt, wk, wvt, w_f, bias, jnp.asarray(code))


def _rope(x, cos, sin_signed):
    return x * cos + pltpu.roll(x, MLA_ROPE, axis=1) * sin_signed


def _rope_t(xt, cos_t, sin_signed_t):
    partner = jnp.concatenate([xt[MLA_ROPE:, :], xt[:MLA_ROPE, :]], axis=0)
    return xt * cos_t + partner * sin_signed_t


def _mla_proj_kernel(x_ref, mod_ref, wa_ref, gq_ref, gkv_ref, wuqt_ref, wkn_ref, wvt_ref,
                     cos_ref, sin_ref, cost_ref, sint_ref, qt_ref, kn_ref, kr_ref, vt_ref):
    h = _mod_norm(x_ref[...], mod_ref).astype(jnp.bfloat16)
    a = _dot(h, wa_ref[...])
    kv0 = MLA_Q_RANK
    kr0 = MLA_Q_RANK + MLA_KV_RANK
    c_q = (_rms(a[:, 0:kv0]) * gq_ref[...]).astype(jnp.bfloat16)
    c_kv = (_rms(a[:, kv0:kr0]) * gkv_ref[...]).astype(jnp.bfloat16)
    kr_ref[...] = _rope(a[:, kr0:kr0 + LANES], cos_ref[...], sin_ref[...]).astype(kr_ref.dtype)
    kn_ref[...] = _dot(c_kv, wkn_ref[...]).astype(kn_ref.dtype)
    vt_ref[...] = _dot_nt(wvt_ref[...], c_kv).astype(vt_ref.dtype)
    scale = (MLA_NOPE + MLA_ROPE) ** -0.5 * LOG2E
    cos_t = cost_ref[...]
    sin_t = sint_ref[...]
    for p in range(N_PAIRS):
        r0 = p * 2 * LANES
        qp = _dot_nt(wuqt_ref[r0:r0 + 2 * LANES, :], c_q)
        qt_ref[r0:r0 + LANES, :] = (qp[0:LANES, :] * scale).astype(qt_ref.dtype)
        qt_ref[r0 + LANES:r0 + 2 * LANES, :] = (
            _rope_t(qp[LANES:2 * LANES, :], cos_t, sin_t) * scale).astype(qt_ref.dtype)


def _mla_weight_layout(w_a, w_uq, w_ukv):
    half = MLA_ROPE // 2
    kr0 = MLA_Q_RANK + MLA_KV_RANK
    x1 = np.arange(half)
    x2 = half + np.arange(half)
    kr_cols = kr0 + np.concatenate([x1, x1, x2, x2] * 2)
    wa_cols = np.concatenate([np.arange(kr0), kr_cols])
    q_cols = []
    qd = MLA_NOPE + MLA_ROPE
    for p in range(N_PAIRS):
        h0, h1 = 2 * p * qd, (2 * p + 1) * qd
        rope = np.concatenate([h0 + MLA_NOPE + x1, h1 + MLA_NOPE + x1,
                               h0 + MLA_NOPE + x2, h1 + MLA_NOPE + x2])
        q_cols.append(np.concatenate([h0 + np.arange(MLA_NOPE), h1 + np.arange(MLA_NOPE),
                                      rope, rope]))
    q_cols = np.concatenate(q_cols)
    kvd = MLA_NOPE + MLA_V
    kn_cols = np.concatenate([hh * kvd + np.arange(MLA_NOPE) for hh in range(N_HEADS)])
    v_cols = np.concatenate([hh * kvd + MLA_NOPE + np.arange(MLA_V) for hh in range(N_HEADS)])
    bf = jnp.bfloat16
    return (w_a[:, wa_cols].astype(bf), w_uq[:, q_cols].T.astype(bf),
            w_ukv[:, kn_cols].astype(bf), w_ukv[:, v_cols].T.astype(bf))


def _rope_tables(seq):
    half = MLA_ROPE // 2
    pos = jnp.arange(seq, dtype=jnp.float32)
    inv_freq = ROPE_BASE ** (-jnp.arange(0, MLA_ROPE, 2, dtype=jnp.float32) / MLA_ROPE)
    ang = pos[:, None] * inv_freq[None, :]
    cos = jnp.tile(jnp.cos(ang), (1, LANES // half))
    sign = np.where((np.arange(LANES) % (4 * half)) < 2 * half, -1.0, 1.0).astype(np.float32)
    sin = jnp.tile(jnp.sin(ang), (1, LANES // half)) * sign[None, :]
    return cos, sin


def _mla_projection(x, mod, w_a, g_q, g_kv, w_uq, w_ukv):
    batch, seq, _ = x.shape
    wa, wuqt, wkn, wvt = _mla_weight_layout(w_a, w_uq, w_ukv)
    cos, sin = _rope_tables(seq)
    tm = PROJ_TM
    d = D_MODEL
    row_spec = lambda width: pl.BlockSpec((None, tm, width), lambda b, i: (b, i, 0))
    col_spec = lambda rows: pl.BlockSpec((None, None, rows, tm), lambda b, i: (b, i, 0, 0))
    full = lambda shape: pl.BlockSpec(shape, lambda b, i: tuple(0 for _ in shape))
    tab = pl.BlockSpec((tm, LANES), lambda b, i: (i, 0))
    tab_t = pl.BlockSpec((LANES, tm), lambda b, i: (0, i))
    bf = jnp.bfloat16
    return pl.pallas_call(
        _mla_proj_kernel,
        grid=(batch, seq // tm),
        in_specs=[
            row_spec(d),
            pl.BlockSpec((None, 1, 3 * d), lambda b, i: (b, 0, 0)),
            full(wa.shape), full((1, MLA_Q_RANK)), full((1, MLA_KV_RANK)),
            full(wuqt.shape), full(wkn.shape), full(wvt.shape), tab, tab, tab_t, tab_t,
        ],
        out_specs=[col_spec(2 * d), row_spec(d), row_spec(LANES), col_spec(d)],
        out_shape=[jax.ShapeDtypeStruct((batch, seq // tm, 2 * d, tm), bf),
                   jax.ShapeDtypeStruct((batch, seq, d), bf),
                   jax.ShapeDtypeStruct((batch, seq, LANES), bf),
                   jax.ShapeDtypeStruct((batch, seq // tm, d, tm), bf)],
        compiler_params=_cparams(("arbitrary", "arbitrary")),
        name="mla_proj",
    )(x, mod, wa, g_q.reshape(1, -1), g_kv.reshape(1, -1), wuqt, wkn, wvt, cos, sin, cos.T, sin.T)


def _head_row_masks(pair, fox):
    row = lax.broadcasted_iota(jnp.int32, (2 * LANES, 1), 0)
    masks = []
    for hh in range(2):
        m = (row >= hh * HEAD_DIM) & (row < (hh + 1) * HEAD_DIM)
        r = row - LANES
        if fox:
            lo = (2 * pair + hh) * EXT_LANES
            m = m | ((r >= lo) & (r < lo + EXT_LANES))
        else:
            in_rope = (r >= 0) & (r < 2 * MLA_ROPE)
            m = m | (in_rope & (((r >> 4) & 1) == hh))
        masks.append(m)
    return masks


def _attn_kernel(*refs, fox):
    if fox:
        qt_ref, qxt_ref, k_ref, kx_ref, vt_ref, o_ref, m_ref, acc_ref, s_ref, cmax_ref = refs
    else:
        qt_ref, k_ref, kx_ref, vt_ref, o_ref, m_ref, acc_ref, s_ref, cmax_ref = refs
    pair = pl.program_id(1)
    qi = pl.program_id(2)
    tk = ATTN_TK
    tq = ATTN_TQ
    nt = tq // tk
    last = nt * qi + nt - 1

    qt = jnp.concatenate([qt_ref[t] for t in range(nt)], axis=1)
    if fox:
        ext = jnp.concatenate([qxt_ref[t] for t in range(nt)], axis=1)
        qt = jnp.concatenate([qt, ext], axis=0)
    q_heads = [jnp.where(m, qt, jnp.zeros_like(qt)) for m in _head_row_masks(pair, fox)]

    m_ref[...] = jnp.full(m_ref.shape, -jnp.inf, jnp.float32)
    acc_ref[...] = jnp.zeros_like(acc_ref)
    vrow = lax.broadcasted_iota(jnp.int32, (LANES, 1), 0)

    qw = ATTN_QW
    n_piece = tq // qw

    def load_keys(j):
        start = pl.multiple_of(j * tk, tk)
        return jnp.concatenate([k_ref[pl.ds(start, tk), :], kx_ref[pl.ds(start, tk), :]], axis=1)

    def load_values(j):
        vt = vt_ref[j]
        ones = jnp.ones_like(vt)
        return [jnp.where(vrow < HEAD_DIM, vt, ones), jnp.where(vrow >= HEAD_DIM, vt, ones)]

    def visible(c, key_offset):
        if key_offset is None or c * qw >= key_offset + tk - 1:
            return "all"
        return "none" if (c + 1) * qw - 1 < key_offset else "some"

    def score_piece(k_t, slot, hh, c, key_offset=None, unmasked=False):
        cols = slice(c * qw, (c + 1) * qw)
        s = _dot(k_t, q_heads[hh][:, cols])
        if visible(c, key_offset) == "some":
            key = lax.broadcasted_iota(jnp.int32, (tk, qw), 0) + key_offset
            qry = lax.broadcasted_iota(jnp.int32, (tk, qw), 1) + c * qw
            s = jnp.where((qry >= key) | unmasked, s, -jnp.inf)
        s_ref[slot, hh, :, cols] = s
        cmax_ref[slot, hh, :, cols] = jnp.max(s, axis=0, keepdims=True)

    def pv_piece(vts, slot, hh, c):
        cols = slice(c * qw, (c + 1) * qw)
        m_old = m_ref[hh, :, cols]
        m_new = jnp.maximum(m_old, cmax_ref[slot, hh, :, cols])
        m_ref[hh, :, cols] = m_new
        alpha = jnp.exp2(m_old - m_new)
        p = jnp.exp2((s_ref[slot, hh, :, cols] - m_new).astype(jnp.bfloat16))
        acc_ref[hh, :, cols] = alpha * acc_ref[hh, :, cols] + _dot(vts[hh], p)

    def advance(j, slot, key_offset=None, next_offset=None):
        k_t = load_keys(j + 1)
        vts = load_values(j)
        for hh in range(2):
            for c in range(n_piece):
                if visible(c, next_offset) != "none":
                    score_piece(k_t, 1 - slot, hh, c, next_offset)
                if visible(c, key_offset) != "none":
                    pv_piece(vts, slot, hh, c)

    k_t = load_keys(0)
    for hh in range(2):
        for c in range(n_piece):
            score_piece(k_t, 0, hh, c, key_offset=0, unmasked=qi > 0)

    def body(i, carry):
        advance(2 * i, 0)
        advance(2 * i + 1, 1)
        return carry

    lax.fori_loop(0, (nt // 2) * qi - 1, body, 0)

    @pl.when(qi > 0)
    def _():
        advance(last - nt - 1, 0)
        advance(last - nt, 1, next_offset=0)

    for t in range(1, nt):
        advance(last - nt + t, (t - 1) % 2, key_offset=(t - 1) * tk, next_offset=t * tk)

    vts = load_values(last)
    for hh in range(2):
        for c in range(n_piece):
            if visible(c, (nt - 1) * tk) != "none":
                pv_piece(vts, (nt - 1) % 2, hh, c)

    a0 = acc_ref[0]
    a1 = acc_ref[1]
    first = vrow < HEAD_DIM
    num = jnp.where(first, a0, a1)
    den = jnp.where(first, a0[HEAD_DIM:HEAD_DIM + 1, :], a1[0:1, :])
    o_ref[...] = (num / den).T.astype(o_ref.dtype)


def _attention(qt, k, kx, vt, *, qxt=None):
    batch, seq, _ = k.shape
    tq, tk = ATTN_TQ, ATTN_TK
    nt = tq // tk
    assert tq == nt * tk and nt % 2 == 0 and tq % ATTN_QW == 0
    assert vt.shape[1:] == (seq // tk, D_MODEL, tk) and qt.shape[3] == tk
    fox = qxt is not None
    rows = qt.shape[2] // N_PAIRS
    shared_rows = lambda rows: pl.BlockSpec((None, rows, LANES), lambda b, p, i: (b, 0, 0))
    in_specs = [pl.BlockSpec((None, nt, rows, tk), lambda b, p, i: (b, i, p, 0))]
    args = [qt]
    if fox:
        in_specs.append(pl.BlockSpec((None, nt, LANES, tk), lambda b, p, i: (b, i, 0, 0)))
        args.append(qxt)
    in_specs += [pl.BlockSpec((None, seq, LANES), lambda b, p, i: (b, 0, p)),
                 shared_rows(seq),
                 pl.BlockSpec((None, seq // tk, LANES, tk), lambda b, p, i: (b, 0, p, 0))]
    args += [k, kx, vt]
    return pl.pallas_call(
        functools.partial(_attn_kernel, fox=fox),
        grid=(batch, N_PAIRS, seq // tq),
        in_specs=in_specs,
        out_specs=pl.BlockSpec((None, tq, LANES), lambda b, p, i: (b, i, p)),
        out_shape=jax.ShapeDtypeStruct((batch, seq, D_MODEL), jnp.bfloat16),
        scratch_shapes=[pltpu.VMEM((2, 1, tq), jnp.float32),
                        pltpu.VMEM((2, LANES, tq), jnp.float32),
                        pltpu.VMEM((2, 2, tk, tq), jnp.float32),
                        pltpu.VMEM((2, 2, 1, tq), jnp.float32)],
        compiler_params=_cparams(("arbitrary", "arbitrary", "arbitrary")),
        name="attn_fox" if fox else "attn_mla",
    )(*args)


def _ffn_kernel(x_ref, o_ref, moda_ref, modf_ref, wo_ref, win_ref, cw_ref, cb_ref, wout_ref, fg_ref,
                out_ref, xn_ref, h_ref, acc_ref, u_ref, carry_ref, *, final_norm):
    i = pl.program_id(1)
    tm = x_ref.shape[0]
    fc = wout_ref.shape[1]
    nf = wout_ref.shape[0]

    gate_a = moda_ref[:, 2 * D_MODEL:3 * D_MODEL]
    xn_ref[...] = x_ref[...] + gate_a * _dot(o_ref[...], wo_ref[...])
    h_ref[...] = _mod_norm(xn_ref[...], modf_ref).astype(h_ref.dtype)

    n_piece = FFN_ROW_PIECES
    rp = tm // n_piece

    def up(f, r):
        rows = slice(r * rp, (r + 1) * rp)
        for part in range(2):
            dst = u_ref.at[f % 2, part]
            if r == 0:
                prev = carry_ref[f, part]
                dst[0:CARRY_ROWS, :] = jnp.where(i == 0, jnp.zeros_like(prev), prev)
            c0 = part * D_FF + f * fc
            dst[CARRY_ROWS + r * rp:CARRY_ROWS + (r + 1) * rp, :] = _dot(
                h_ref[rows, :], win_ref[:, c0:c0 + fc])
            if r == n_piece - 1:
                carry_ref[f, part] = dst[tm:tm + CARRY_ROWS, :]

    def down(f, r):
        ys = []
        for part in range(2):
            src = u_ref.at[f % 2, part]
            c0 = part * D_FF + f * fc
            y = cb_ref[:, c0:c0 + fc]
            for tap in range(CONV_WIDTH):
                off = CARRY_ROWS - (CONV_WIDTH - 1) + tap + r * rp
                y = y + cw_ref[tap:tap + 1, c0:c0 + fc] * src[off:off + rp, :]
            ys.append(y)
        yg, yv = ys
        act = ((yg * (1.0 / (1.0 + jnp.exp(-yg)))) * yv).astype(jnp.bfloat16)
        rows = slice(r * rp, (r + 1) * rp)
        d = _dot(act, wout_ref[f])
        if f == 0:
            acc_ref[rows, :] = d
        else:
            acc_ref[rows, :] += d

    for r in range(n_piece):
        up(0, r)
    for f in range(nf):
        for r in range(n_piece):
            if f + 1 < nf:
                up(f + 1, r)
            down(f, r)

    gate_f = modf_ref[:, 2 * D_MODEL:3 * D_MODEL]
    y = xn_ref[...] + gate_f * acc_ref[...]
    if final_norm:
        y = _rms(y) * fg_ref[...]
    out_ref[...] = y


def _mixer_out_and_ffn(x, o, mod_a, mod_f, w_o, w_in, conv_w, conv_b, w_out, final_g, final_norm):
    batch, seq, _ = x.shape
    tm, fc = FFN_TM, FFN_FC
    nf = D_FF // fc
    bf = jnp.bfloat16
    d = D_MODEL
    w_out_c = w_out.astype(bf).reshape(nf, fc, d)
    row = lambda: pl.BlockSpec((None, tm, d), lambda b, i: (b, i, 0))
    mod_spec = lambda: pl.BlockSpec((None, 1, 3 * d), lambda b, i: (b, 0, 0))
    whole = lambda shape: pl.BlockSpec(shape, lambda b, i: tuple(0 for _ in shape),
                                       pipeline_mode=pl.Buffered(1))
    return pl.pallas_call(
        functools.partial(_ffn_kernel, final_norm=final_norm),
        grid=(batch, seq // tm),
        in_specs=[
            row(), row(), mod_spec(), mod_spec(),
            whole((d, d)), whole((d, 2 * D_FF)), whole((CONV_WIDTH, 2 * D_FF)),
            whole((1, 2 * D_FF)), whole((nf, fc, d)), whole((1, d)),
        ],
        out_specs=row(),
        out_shape=jax.ShapeDtypeStruct((batch, seq, d), jnp.float32),
        scratch_shapes=[
            pltpu.VMEM((tm, d), jnp.float32),
            pltpu.VMEM((tm, d), bf),
            pltpu.VMEM((tm, d), jnp.float32),
            pltpu.VMEM((2, 2, tm + CARRY_ROWS, fc), jnp.float32),
            pltpu.VMEM((nf, 2, CARRY_ROWS, fc), jnp.float32),
        ],
        compiler_params=_cparams(("arbitrary", "arbitrary")),
        name="mixer_out_ffn",
    )(x, o, mod_a, mod_f, w_o.astype(bf), w_in.astype(bf), conv_w, conv_b.reshape(1, -1), w_out_c,
      final_g.reshape(1, -1))


def kernel(x, c, ada_w, ada_b, fox_w_in, fox_b_f, fox_w_o, mla_w_a, mla_g_q, mla_g_kv, mla_w_uq,
           mla_w_ukv, mla_w_o, ffn_w_in, ffn_conv_w, ffn_conv_b, ffn_w_out, final_g):
    depth = ada_w.shape[0]
    mods = _ada_modulation(c, ada_w, ada_b)
    for layer in range(depth):
        j = layer // 2
        mod_a, mod_f = mods[2 * layer], mods[2 * layer + 1]
        if layer % 2 == 0:
            qt, k, vt, qxt, kx = _fox_projection(x, mod_a, fox_w_in[j], fox_b_f[j])
            o = _attention(qt, k, kx, vt, qxt=qxt)
            w_o = fox_w_o[j]
        else:
            qt, kn, kr, vt = _mla_projection(x, mod_a, mla_w_a[j], mla_g_q[j], mla_g_kv[j],
                                             mla_w_uq[j], mla_w_ukv[j])
            o = _attention(qt, kn, kr, vt)
            w_o = mla_w_o[j]
        x = _mixer_out_and_ffn(x, o, mod_a, mod_f, w_o, ffn_w_in[layer], ffn_conv_w[layer],
                               ffn_conv_b[layer], ffn_w_out[layer], final_g,
                               final_norm=(layer == depth - 1))
    return x
```
